```python
import jax
import jax.numpy as jnp
from jax import lax
import numpy as np

D_MODEL = 2048
BATCH = 4
SEQ = 8192
DEPTH = 2
DEC_BATCH = 8
DEC_SEQ = 2048
PAST_LEN = 128

MIX_WIDTH = D_MODEL
HG_WIDTH = MIX_WIDTH // 2
POOL_WIDTH = MIX_WIDTH - HG_WIDTH
HG_HEAD_K = 128
HG_HEADS = HG_WIDTH // HG_HEAD_K
HG_HEAD_V = HG_WIDTH // HG_HEADS
POOL_WINDOWS = (2, 4, 8, 16)
POOL_GROUPS = len(POOL_WINDOWS)
POOL_GROUP_WIDTH = POOL_WIDTH // POOL_GROUPS
HG_COLS = 5 * HG_WIDTH
IN_COLS = HG_COLS + POOL_WIDTH
D_FF = 4 * D_MODEL
N_META = 16
CHUNK = 64
META_PAD = (-N_META) % CHUNK
EPS = 1e-6
FORGET_FLOOR = 1e-30

kernel_name = 'hymba_hgrn2_pool_bidir_encoder'


def rms_norm(x, gain):
    xf = x.astype(jnp.float32)
    xf = xf * lax.rsqrt(jnp.mean(xf * xf, axis=-1, keepdims=True) + EPS)
    return (xf * gain.astype(jnp.float32)).astype(x.dtype)


def gla_chunk_scan(q, k, v, log_f):
    B, L, H, DK = q.shape
    DV = v.shape[-1]
    n_chunks = L // CHUNK

    def to_chunks(a):
        return a.reshape(B, n_chunks, CHUNK, H, a.shape[-1]).transpose(1, 0, 3, 2, 4)

    inclusive = jnp.tril(jnp.ones((CHUNK, CHUNK), dtype=bool))[:, :, None]

    def step(state, xs):
        qc, kc, vc, gc = xs
        b = jnp.cumsum(gc, axis=2)
        rel = jnp.where(inclusive, b[:, :, :, None, :] - b[:, :, None, :, :], 0.0)
        decay = jnp.where(inclusive, jnp.exp(rel), 0.0)
        scores = jnp.einsum('bhtd,bhsd,bhtsd->bhts', qc, kc, decay)
        out = (jnp.einsum('bhts,bhsv->bhtv', scores, vc)
               + jnp.einsum('bhtd,bhdv->bhtv', qc * jnp.exp(b), state))
        b_last = b[:, :, -1:, :]
        state = (state * jnp.exp(b_last[:, :, 0, :, None])
                 + jnp.einsum('bhsd,bhsv->bhdv', kc * jnp.exp(b_last - b), vc))
        return state, out

    state0 = jnp.zeros((B, H, DK, DV), jnp.float32)
    _, out = lax.scan(step, state0, (to_chunks(q), to_chunks(k), to_chunks(v), to_chunks(log_f)))
    return out.transpose(1, 0, 3, 2, 4).reshape(B, L, H, DV)


def hgrn2_mixer(u, lb_fwd, lb_bwd, head_norm):
    B, L, _ = u.shape
    uf = u.astype(jnp.float32)
    q, f_fwd, f_bwd, inp, gate = jnp.split(uf, 5, axis=-1)
    q = jax.nn.silu(q)

    def forget(f_pre, lb):
        sig = jax.nn.sigmoid(f_pre)
        f = lb + (1.0 - lb) * sig
        log_f = jnp.log(jnp.maximum(f, FORGET_FLOOR))
        k = (1.0 - lb) * (1.0 - sig)
        return log_f, k

    lf_fwd, k_fwd = forget(f_fwd, lb_fwd)
    lf_bwd, k_bwd = forget(f_bwd, lb_bwd)

    def heads_padded(a):
        a = a.reshape(B, L, HG_HEADS, -1)
        return jnp.pad(a, ((0, 0), (META_PAD, 0), (0, 0), (0, 0)))

    q, inp, lf_fwd, k_fwd, lf_bwd, k_bwd = map(heads_padded, (q, inp, lf_fwd, k_fwd, lf_bwd, k_bwd))
    rev = lambda a: jnp.flip(a, axis=1)
    o = (gla_chunk_scan(q, k_fwd, inp, lf_fwd)
         + rev(gla_chunk_scan(rev(q), rev(k_bwd), rev(inp), rev(lf_bwd))))
    o = o[:, META_PAD:]
    o = (o * lax.rsqrt(jnp.mean(o * o, axis=-1, keepdims=True) + EPS)
         * head_norm.astype(jnp.float32).reshape(HG_HEADS, HG_HEAD_V))
    return o.reshape(B, L, HG_WIDTH) * jax.nn.silu(gate)


def pool_mixer(u, w_pool, pool_scale):
    B, L, _ = u.shape
    uf = u.astype(jnp.float32).reshape(B, L, POOL_GROUPS, POOL_GROUP_WIDTH)
    csum = jnp.pad(jnp.cumsum(uf, axis=1), ((0, 0), (1, 0), (0, 0), (0, 0)))
    pos = jnp.arange(L)
    pooled = []
    for g, window in enumerate(POOL_WINDOWS):
        lo = jnp.clip(pos - window // 2, 0, L)
        hi = jnp.clip(pos + window - window // 2, 0, L)
        total = jnp.take(csum[:, :, g], hi, axis=1) - jnp.take(csum[:, :, g], lo, axis=1)
        mean = total / (hi - lo).astype(jnp.float32)[None, :, None]
        pooled.append(mean - uf[:, :, g])
    pooled = jnp.stack(pooled, axis=2)
    y = jnp.einsum('blgc,gcd->blgd', pooled, w_pool.astype(jnp.float32))
    return y.reshape(B, L, POOL_WIDTH) * pool_scale.astype(jnp.float32)


def encoder_layer(h, w_in, w_pool, pool_scale, lb_fwd, lb_bwd, head_norm, w_out,
                  norm_mix, norm_mlp, w_up, w_down):
    a = rms_norm(h, norm_mix)
    u = jnp.einsum('bld,dc->blc', a, w_in.astype(h.dtype))
    y_hg = hgrn2_mixer(u[..., :HG_COLS], lb_fwd, lb_bwd, head_norm)
    y_pool = pool_mixer(u[..., HG_COLS:], w_pool, pool_scale)
    mixed = jnp.concatenate([y_hg, y_pool], axis=-1).astype(h.dtype)
    h = h + jnp.einsum('blc,cd->bld', mixed, w_out.astype(h.dtype))
    m = rms_norm(h, norm_mlp)
    hidden = jnp.square(jax.nn.relu(jnp.einsum('bld,df->blf', m, w_up.astype(h.dtype))))
    return h + jnp.einsum('blf,fd->bld', hidden, w_down.astype(h.dtype))


def encoder_trunk(x, meta_tokens, lower, w_in, w_pool, pool_scale, hg_head_norm, w_out,
                  norm_mix, norm_mlp, w_up, w_down, final_norm):
    B = x.shape[0]
    meta = jnp.broadcast_to(meta_tokens.astype(x.dtype)[None], (B, N_META, D_MODEL))
    h = jnp.concatenate([meta, x], axis=1)
    for l in range(DEPTH):
        h = encoder_layer(h, w_in[l], w_pool[l], pool_scale[l], lower[0, l], lower[1, l],
                          hg_head_norm[l], w_out[l], norm_mix[l], norm_mlp[l], w_up[l], w_down[l])
    return rms_norm(h, final_norm)[:, N_META:]


def setup_inputs(seed: int = 0) -> dict:
    key = jax.random.key(seed)
    ks = jax.random.split(key, 14)
    f32 = jnp.float32

    def nrm(k, shape, scale):
        return jax.random.normal(k, shape, f32) * scale

    return {
        'x_prompt': nrm(ks[0], (BATCH, SEQ, D_MODEL), 1.0),
        'x_sample': nrm(ks[1], (DEC_BATCH, DEC_SEQ, D_MODEL), 1.0),
        'meta_tokens': nrm(ks[2], (N_META, D_MODEL), 1.0),
        'w_in': nrm(ks[3], (DEPTH, D_MODEL, IN_COLS), D_MODEL ** -0.5),
        'w_pool': nrm(ks[4], (DEPTH, POOL_GROUPS, POOL_GROUP_WIDTH, POOL_GROUP_WIDTH), POOL_GROUP_WIDTH ** -0.5),
        'pool_scale': 1.0 + nrm(ks[5], (DEPTH, POOL_WIDTH), 0.1),
        'hg_lower_bound': nrm(ks[6], (2, DEPTH, HG_WIDTH), 1.0),
        'hg_head_norm': 1.0 + nrm(ks[7], (DEPTH, HG_WIDTH), 0.1),
        'w_out': nrm(ks[8], (DEPTH, MIX_WIDTH, D_MODEL), MIX_WIDTH ** -0.5),
        'norm_mix': 1.0 + nrm(ks[9], (DEPTH, D_MODEL), 0.1),
        'norm_mlp': 1.0 + nrm(ks[10], (DEPTH, D_MODEL), 0.1),
        'w_up': nrm(ks[11], (DEPTH, D_MODEL, D_FF), D_MODEL ** -0.5),
        'w_down': nrm(ks[12], (DEPTH, D_FF, D_MODEL), D_FF ** -0.5),
        'final_norm': 1.0 + nrm(ks[13], (D_MODEL,), 0.1),
    }


def reference(x_prompt, x_sample, meta_tokens, w_in, w_pool, pool_scale, hg_lower_bound,
              hg_head_norm, w_out, norm_mix, norm_mlp, w_up, w_down, final_norm):
    probs = jax.nn.softmax(hg_lower_bound.astype(jnp.float32), axis=1)
    lower = jnp.cumsum(probs, axis=1) - probs[:, :1]
    y_prompt = encoder_trunk(x_prompt, meta_tokens, lower, w_in, w_pool, pool_scale, hg_head_norm,
                             w_out, norm_mix, norm_mlp, w_up, w_down, final_norm)
    y_sample = encoder_trunk(x_sample, meta_tokens, lower, w_in, w_pool, pool_scale, hg_head_norm,
                             w_out, norm_mix, norm_mlp, w_up, w_down, final_norm)
    return (y_prompt, y_sample)
```

```python
import collections
import functools

import jax
import jax.numpy as jnp
from jax import lax
from jax.experimental import pallas as pl
from jax.experimental.pallas import tpu as pltpu

N_META = 16
CHUNK = 64
META_PAD = (-N_META) % CHUNK
HEAD = 128
SUBLANES = 8
POOL_WINDOWS = (2, 4, 8, 16)
POOL_HALO = 8
EPS = 1e-6
FORGET_FLOOR = 1e-30
SAFE_LOG_DECAY = 60.0

VMEM_LIMIT_BYTES = 56 * 1024 * 1024

F32 = jnp.float32
BF16 = jnp.bfloat16

Layout = collections.namedtuple("Layout", "n_a cpa n_b cpb")


def _num_chunks(lay):
    return lay.n_a * lay.cpa + lay.n_b * lay.cpb


def _chunk_in_seq(cg, lay):
    na = lay.n_a * lay.cpa
    in_a = cg < na
    idx = jnp.where(in_a, lax.rem(cg, lay.cpa), lax.rem(jnp.maximum(cg - na, 0), lay.cpb))
    cps = jnp.where(in_a, lay.cpa, lay.cpb)
    return idx, cps


def _largest_divisor(n, cap):
    best = 1
    for d in range(1, n + 1):
        if n % d == 0 and d <= cap:
            best = d
    return best


def _rms(x, gain):
    ms = jnp.mean(x * x, axis=-1, keepdims=True)
    return x * lax.rsqrt(ms + EPS) * gain


def _sigmoid(x):
    return 1.0 / (1.0 + jnp.exp(-x))


def _dot(a, b):
    return jnp.dot(a, b, preferred_element_type=F32)


def _dot_nt(a, b):
    return lax.dot_general(a, b, (((1,), (1,)), ((), ())), preferred_element_type=F32)


def _dot_tn(a, b):
    return lax.dot_general(a, b, (((0,), (0,)), ((), ())), preferred_element_type=F32)


def _params(sem):
    return pltpu.CompilerParams(dimension_semantics=sem, vmem_limit_bytes=VMEM_LIMIT_BYTES)


def _norm_proj_kernel(h_ref, g_ref, w_ref, o_ref, a_scr):
    @pl.when(pl.program_id(1) == 0)
    def _():
        a_scr[...] = _rms(h_ref[...], g_ref[...]).astype(BF16)

    o_ref[...] = _dot(a_scr[...], w_ref[...])


def _norm_proj(h, gain, w, *, bm, bn):
    rows, d = h.shape
    n = w.shape[1]
    return pl.pallas_call(
        _norm_proj_kernel,
        grid=(rows // bm, n // bn),
        in_specs=[
            pl.BlockSpec((bm, d), lambda i, j: (i, 0)),
            pl.BlockSpec((1, d), lambda i, j: (0, 0)),
            pl.BlockSpec((d, bn), lambda i, j: (0, j)),
        ],
        out_specs=pl.BlockSpec((bm, bn), lambda i, j: (i, j)),
        out_shape=jax.ShapeDtypeStruct((rows, n), F32),
        scratch_shapes=[pltpu.VMEM((bm, d), BF16)],
        compiler_params=_params(("parallel", "arbitrary")),
        name="norm_proj",
    )(h, gain, w)


def _forget(f_pre, lb):
    sig = _sigmoid(f_pre)
    f = lb + (1.0 - lb) * sig
    log_f = jnp.log(jnp.maximum(f, FORGET_FLOOR))
    k = (1.0 - lb) * (1.0 - sig)
    return log_f, k


def _tri(lower):
    r = lax.broadcasted_iota(jnp.int32, (CHUNK, CHUNK), 0)
    c = lax.broadcasted_iota(jnp.int32, (CHUNK, CHUNK), 1)
    return (r >= c) if lower else (r <= c)


def _cumsum_rows(tri_bf16, g):
    hi = g.astype(BF16)
    lo = (g - hi.astype(F32)).astype(BF16)
    return _dot(tri_bf16, hi) + _dot(tri_bf16, lo)


def _pad_row_mask(is_first):
    r = lax.broadcasted_iota(jnp.int32, (CHUNK, 1), 0)
    return jnp.logical_or(jnp.logical_not(is_first), r >= META_PAD)


def _bwd_state_kernel(fb_ref, ip_ref, lb_ref, sb_ref, st_scr, *, lay, cpt, heads):
    tile = pl.num_programs(0) - 1 - pl.program_id(0)
    upper = jnp.where(_tri(False), 1.0, 0.0).astype(BF16)
    lb = lb_ref[...]

    def chunk_body(cc, carry):
        c = cpt - 1 - cc
        idx, cps = _chunk_in_seq(tile * cpt + c, lay)

        @pl.when(idx == cps - 1)
        def _():
            st_scr[...] = jnp.zeros_like(st_scr)

        sb_ref[c] = st_scr[...].astype(BF16)

        rows = pl.ds(pl.multiple_of(c * CHUNK, CHUNK), CHUNK)
        valid = _pad_row_mask(idx == 0)
        v16 = jnp.where(valid, ip_ref[rows, :], 0.0).astype(BF16)
        g, k = _forget(fb_ref[rows, :], lb)
        cb = _cumsum_rows(upper, g)
        c0 = cb[0:1, :]
        khat = (k * jnp.exp(c0 - cb)).astype(BF16)
        decayed = st_scr[...] * jnp.exp(c0)
        for h in range(heads):
            hs = slice(h * HEAD, (h + 1) * HEAD)
            st_scr[:, hs] = decayed[:, hs] + _dot_tn(v16[:, hs], khat[:, hs])
        return carry

    lax.fori_loop(0, cpt, chunk_body, 0)


def _bwd_states(u, lb_b, *, lay, cpt, hgw):
    nc = _num_chunks(lay)
    nt = nc // cpt
    tr = cpt * CHUNK
    heads = hgw // HEAD
    kern = functools.partial(_bwd_state_kernel, lay=lay, cpt=cpt, heads=heads)
    return pl.pallas_call(
        kern,
        grid=(nt,),
        in_specs=[
            pl.BlockSpec((tr, hgw), lambda j: (nt - 1 - j, 2)),
            pl.BlockSpec((tr, hgw), lambda j: (nt - 1 - j, 3)),
            pl.BlockSpec((1, hgw), lambda j: (0, 0)),
        ],
        out_specs=pl.BlockSpec((cpt, HEAD, hgw), lambda j: (nt - 1 - j, 0, 0)),
        out_shape=jax.ShapeDtypeStruct((nc, HEAD, hgw), BF16),
        scratch_shapes=[pltpu.VMEM((HEAD, hgw), F32)],
        compiler_params=_params(("arbitrary",)),
        name="hgrn2_bwd_states",
    )(u, u, lb_b)


def _exact_scores(q_scr, k_scr, b_scr, hs, mask):
    qh = q_scr[:, hs]
    bh = b_scr[:, hs]
    lane = lax.broadcasted_iota(jnp.int32, (CHUNK, CHUNK), 1)

    def body(sg, acc):
        group = pl.ds(pl.multiple_of(sg * SUBLANES, SUBLANES), SUBLANES)
        k8 = k_scr[group, hs]
        b8 = b_scr[group, hs]
        for j in range(SUBLANES):
            e = jnp.exp(jnp.minimum(bh - b8[j:j + 1, :], 0.0))
            col = jnp.sum(qh * k8[j:j + 1, :] * e, axis=1, keepdims=True)
            acc = jnp.where(lane == sg * SUBLANES + j, col, acc)
        return acc

    acc = lax.fori_loop(0, CHUNK // SUBLANES, body, jnp.zeros((CHUNK, CHUNK), F32))
    return jnp.where(mask, acc, 0.0)


def _hgrn2_kernel(q_ref, ff_ref, fb_ref, ip_ref, gt_ref, sb_ref, lb_ref, hn_ref, o_ref,
                  st_scr, a_scr, q_scr, kf_scr, bf_scr, kb_scr, cb_scr, *, lay, cpt, heads):
    tile = pl.program_id(0)
    lower_m = _tri(True)
    upper_m = _tri(False)
    lower = jnp.where(lower_m, 1.0, 0.0).astype(BF16)
    upper = jnp.where(upper_m, 1.0, 0.0).astype(BF16)
    lb_f = lb_ref[0:1, :]
    lb_b = lb_ref[1:2, :]
    half = CHUNK // 2

    def chunk_body(c, carry):
        idx, _ = _chunk_in_seq(tile * cpt + c, lay)
        is_first = idx == 0

        @pl.when(is_first)
        def _():
            st_scr[...] = jnp.zeros_like(st_scr)

        rows = pl.ds(pl.multiple_of(c * CHUNK, CHUNK), CHUNK)
        valid = _pad_row_mask(is_first)
        q_pre = q_ref[rows, :]
        q = q_pre * _sigmoid(q_pre)
        v16 = jnp.where(valid, ip_ref[rows, :], 0.0).astype(BF16)
        gf, kf = _forget(ff_ref[rows, :], lb_f)
        gb, kb = _forget(fb_ref[rows, :], lb_b)
        bf = _cumsum_rows(lower, gf)
        cb = _cumsum_rows(upper, gb)
        b_last = bf[CHUNK - 1:CHUNK, :]
        c_first = cb[0:1, :]
        rf = bf[half - 1:half, :]
        rb = cb[half:half + 1, :]
        worst = jnp.minimum(jnp.minimum(jnp.min(rf), jnp.min(b_last - rf)),
                            jnp.minimum(jnp.min(rb), jnp.min(c_first - rb)))
        fast = worst >= -SAFE_LOG_DECAY

        @pl.when(fast)
        def _():
            qtf = (q * jnp.exp(bf - rf)).astype(BF16)
            ktf = (kf * jnp.exp(rf - bf)).astype(BF16)
            qtb = (q * jnp.exp(cb - rb)).astype(BF16)
            ktb = (kb * jnp.exp(rb - cb)).astype(BF16)
            for h in range(heads):
                hs = slice(h * HEAD, (h + 1) * HEAD)
                af = _dot_nt(qtf[:, hs], ktf[:, hs])
                ab = _dot_nt(qtb[:, hs], ktb[:, hs])
                a_scr[h] = jnp.where(lower_m, af, 0.0) + jnp.where(upper_m, ab, 0.0)

        @pl.when(jnp.logical_not(fast))
        def _():
            q_scr[...] = q
            kf_scr[...] = kf
            bf_scr[...] = bf
            kb_scr[...] = kb
            cb_scr[...] = cb
            for h in range(heads):
                hs = slice(h * HEAD, (h + 1) * HEAD)
                a_scr[h] = (_exact_scores(q_scr, kf_scr, bf_scr, hs, lower_m)
                            + _exact_scores(q_scr, kb_scr, cb_scr, hs, upper_m))

        qhf = (q * jnp.exp(bf)).astype(BF16)
        qhb = (q * jnp.exp(cb)).astype(BF16)
        khat = (kf * jnp.exp(b_last - bf)).astype(BF16)
        st_old = st_scr[...]
        st16 = st_old.astype(BF16)
        sb16 = sb_ref[c]
        decayed = st_old * jnp.exp(b_last)
        gate_pre = gt_ref[rows, :]
        gate = gate_pre * _sigmoid(gate_pre)
        hn = hn_ref[...]
        for h in range(heads):
            hs = slice(h * HEAD, (h + 1) * HEAD)
            qcat = jnp.concatenate([qhf[:, hs], qhb[:, hs]], axis=1)
            scat = jnp.concatenate([st16[:, hs], sb16[:, hs]], axis=1)
            o = _dot(a_scr[h].astype(BF16), v16[:, hs]) + _dot_nt(qcat, scat)
            ms = jnp.mean(o * o, axis=-1, keepdims=True)
            y = o * lax.rsqrt(ms + EPS) * hn[:, hs] * gate[:, hs]
            o_ref[rows, hs] = jnp.where(valid, y, 0.0).astype(o_ref.dtype)
            st_scr[:, hs] = decayed[:, hs] + _dot_tn(v16[:, hs], khat[:, hs])
        return carry

    lax.fori_loop(0, cpt, chunk_body, 0)


def _hgrn2(u, sb, lb, head_norm, *, lay, cpt, hgw):
    nc = _num_chunks(lay)
    nt = nc // cpt
    tr = cpt * CHUNK
    heads = hgw // HEAD
    kern = functools.partial(_hgrn2_kernel, lay=lay, cpt=cpt, heads=heads)
    col = lambda part: pl.BlockSpec((tr, hgw), lambda i: (i, part))
    chunk_f32 = pltpu.VMEM((CHUNK, hgw), F32)
    return pl.pallas_call(
        kern,
        grid=(nt,),
        in_specs=[
            col(0), col(1), col(2), col(3), col(4),
            pl.BlockSpec((cpt, HEAD, hgw), lambda i: (i, 0, 0)),
            pl.BlockSpec((2, hgw), lambda i: (0, 0)),
            pl.BlockSpec((1, hgw), lambda i: (0, 0)),
        ],
        out_specs=pl.BlockSpec((tr, hgw), lambda i: (i, 0)),
        out_shape=jax.ShapeDtypeStruct((nc * CHUNK, hgw), BF16),
        scratch_shapes=[
            pltpu.VMEM((HEAD, hgw), F32),
            pltpu.VMEM((heads, CHUNK, CHUNK), F32),
            chunk_f32, chunk_f32, chunk_f32, chunk_f32, chunk_f32,
        ],
        compiler_params=_params(("arbitrary",)),
        name="hgrn2_scan",
    )(u, u, u, u, u, sb, lb, head_norm)


def _pool_kernel(prev_ref, x_ref, next_ref, wp_ref, ps_ref, o_ref, xs_scr, *, lay, tm):
    idx, cps = _chunk_in_seq(pl.program_id(0) * (tm // CHUNK), lay)
    p0 = idx * CHUNK
    seq_rows = cps * CHUNK
    pos_h = p0 - POOL_HALO + lax.broadcasted_iota(jnp.int32, (tm + 2 * POOL_HALO, 1), 0)
    valid_h = jnp.logical_and(pos_h >= META_PAD, pos_h < seq_rows)
    x_all = jnp.concatenate([prev_ref[...], x_ref[...], next_ref[...]], axis=0)
    xs_scr[...] = jnp.where(valid_h, x_all, 0.0)

    pos = p0 + lax.broadcasted_iota(jnp.int32, (tm, 1), 0)
    valid = pos >= META_PAD
    gw = x_ref.shape[1] // len(POOL_WINDOWS)
    for g, window in enumerate(POOL_WINDOWS):
        cols = slice(g * gw, (g + 1) * gw)
        back = window // 2
        total = xs_scr[pl.ds(POOL_HALO - back, tm), cols]
        for j in range(1, window):
            total = total + xs_scr[pl.ds(POOL_HALO - back + j, tm), cols]
        count = (jnp.minimum(pos + (window - back), seq_rows) - jnp.maximum(pos - back, META_PAD))
        count = jnp.maximum(count, 1).astype(F32)
        pooled = total / count - xs_scr[pl.ds(POOL_HALO, tm), cols]
        y = _dot(pooled.astype(BF16), wp_ref[g]) * ps_ref[:, cols]
        o_ref[:, cols] = jnp.where(valid, y, 0.0).astype(o_ref.dtype)


def _pool(u, w_pool, pool_scale, *, lay, tm, hgw):
    rows = u.shape[0]
    pw = pool_scale.shape[1]
    col = 5 * hgw // pw
    hb = tm // POOL_HALO
    last_hb = rows // POOL_HALO - 1
    kern = functools.partial(_pool_kernel, lay=lay, tm=tm)
    return pl.pallas_call(
        kern,
        grid=(rows // tm,),
        in_specs=[
            pl.BlockSpec((POOL_HALO, pw), lambda i: (jnp.maximum(i * hb - 1, 0), col)),
            pl.BlockSpec((tm, pw), lambda i: (i, col)),
            pl.BlockSpec((POOL_HALO, pw), lambda i: (jnp.minimum((i + 1) * hb, last_hb), col)),
            pl.BlockSpec(w_pool.shape, lambda i: (0, 0, 0)),
            pl.BlockSpec((1, pw), lambda i: (0, 0)),
        ],
        out_specs=pl.BlockSpec((tm, pw), lambda i: (i, 0)),
        out_shape=jax.ShapeDtypeStruct((rows, pw), BF16),
        scratch_shapes=[pltpu.VMEM((tm + 2 * POOL_HALO, pw), F32)],
        compiler_params=_params(("parallel",)),
        name="pool_mixer",
    )(u, u, u, w_pool, pool_scale)


def _out_proj_kernel(h_ref, yh_ref, yp_ref, w_ref, o_ref):
    hgw = yh_ref.shape[1]
    o_ref[...] = (h_ref[...] + _dot(yh_ref[...], w_ref[0:hgw, :])
                  + _dot(yp_ref[...], w_ref[hgw:, :]))


def _out_proj(h, y_hg, y_pool, w_out, *, bm):
    rows, d = h.shape
    hgw, pw = y_hg.shape[1], y_pool.shape[1]
    return pl.pallas_call(
        _out_proj_kernel,
        grid=(rows // bm,),
        in_specs=[
            pl.BlockSpec((bm, d), lambda i: (i, 0)),
            pl.BlockSpec((bm, hgw), lambda i: (i, 0)),
            pl.BlockSpec((bm, pw), lambda i: (i, 0)),
            pl.BlockSpec(w_out.shape, lambda i: (0, 0)),
        ],
        out_specs=pl.BlockSpec((bm, d), lambda i: (i, 0)),
        out_shape=jax.ShapeDtypeStruct((rows, d), F32),
        compiler_params=_params(("parallel",)),
        name="out_proj",
    )(h, y_hg, y_pool, w_out)


def _mlp_kernel(h_ref, g_ref, wu_ref, wd_ref, fg_ref, o_ref, m_scr, *, final):
    f = pl.program_id(1)

    @pl.when(f == 0)
    def _():
        m_scr[...] = _rms(h_ref[...], g_ref[...]).astype(BF16)

    hidden = jnp.square(jnp.maximum(_dot(m_scr[...], wu_ref[...]), 0.0)).astype(BF16)
    part = _dot(hidden, wd_ref[...])

    @pl.when(f == 0)
    def _():
        o_ref[...] = h_ref[...] + part

    @pl.when(f > 0)
    def _():
        o_ref[...] += part

    if final:
        @pl.when(f == pl.num_programs(1) - 1)
        def _():
            o_ref[...] = _rms(o_ref[...], fg_ref[...])


def _mlp(h, gain, w_up, w_down, final_gain, *, bm, bf, final):
    rows, d = h.shape
    dff = w_up.shape[1]
    return pl.pallas_call(
        functools.partial(_mlp_kernel, final=final),
        grid=(rows // bm, dff // bf),
        in_specs=[
            pl.BlockSpec((bm, d), lambda i, f: (i, 0)),
            pl.BlockSpec((1, d), lambda i, f: (0, 0)),
            pl.BlockSpec((d, bf), lambda i, f: (0, f)),
            pl.BlockSpec((bf, d), lambda i, f: (f, 0)),
            pl.BlockSpec((1, d), lambda i, f: (0, 0)),
        ],
        out_specs=pl.BlockSpec((bm, d), lambda i, f: (i, 0)),
        out_shape=jax.ShapeDtypeStruct((rows, d), F32),
        scratch_shapes=[pltpu.VMEM((bm, d), BF16)],
        compiler_params=_params(("parallel", "arbitrary")),
        name="mlp",
    )(h, gain, w_up, w_down, final_gain)


def _pack_rows(x, meta):
    b, _, d = x.shape
    pad = jnp.zeros((b, META_PAD, d), x.dtype)
    m = jnp.broadcast_to(meta.astype(x.dtype)[None], (b, N_META, d))
    return jnp.concatenate([pad, m, x], axis=1).reshape(-1, d)


def _tile_plan(lay):
    nc = _num_chunks(lay)
    g = 1
    for d in range(1, min(lay.cpa, lay.cpb) + 1):
        if lay.cpa % d == 0 and lay.cpb % d == 0:
            g = d
    return dict(
        proj_bm=CHUNK * _largest_divisor(nc, 12),
        scan_cpt=_largest_divisor(nc, 6),
        pool_tm=CHUNK * _largest_divisor(g, 4),
        out_bm=CHUNK * _largest_divisor(nc, 10),
        mlp_bm=CHUNK * _largest_divisor(nc, 10),
    )


def kernel(x_prompt, x_sample, meta_tokens, w_in, w_pool, pool_scale, hg_lower_bound, hg_head_norm,
           w_out, norm_mix, norm_mlp, w_up, w_down, final_norm):
    depth, d, in_cols = w_in.shape
    pw = pool_scale.shape[1]
    hgw = hg_head_norm.shape[1]
    dff = w_up.shape[2]
    assert in_cols == 5 * hgw + pw and hgw % HEAD == 0 and pw == hgw
    assert (x_prompt.shape[1] + N_META + META_PAD) % CHUNK == 0
    assert (x_sample.shape[1] + N_META + META_PAD) % CHUNK == 0
    lay = Layout(x_prompt.shape[0], (x_prompt.shape[1] + N_META + META_PAD) // CHUNK,
                 x_sample.shape[0], (x_sample.shape[1] + N_META + META_PAD) // CHUNK)
    plan = _tile_plan(lay)
    proj_bn = in_cols // _largest_divisor(in_cols // 256, 4)
    mlp_bf = dff // _largest_divisor(dff // 256, 8)

    probs = jax.nn.softmax(hg_lower_bound.astype(F32), axis=1)
    lower = jnp.cumsum(probs, axis=1) - probs[:, :1]

    h = jnp.concatenate([_pack_rows(x_prompt, meta_tokens), _pack_rows(x_sample, meta_tokens)], axis=0)
    row = lambda a: a.astype(F32).reshape(1, -1)
    for l in range(depth):
        u = _norm_proj(h, row(norm_mix[l]), w_in[l].astype(BF16), bm=plan["proj_bm"], bn=proj_bn)
        sb = _bwd_states(u, lower[1:2, l], lay=lay, cpt=plan["scan_cpt"], hgw=hgw)
        y_hg = _hgrn2(u, sb, lower[:, l], row(hg_head_norm[l]), lay=lay, cpt=plan["scan_cpt"], hgw=hgw)
        y_pool = _pool(u, w_pool[l].astype(BF16), row(pool_scale[l]), lay=lay, tm=plan["pool_tm"], hgw=hgw)
        h = _out_proj(h, y_hg, y_pool, w_out[l].astype(BF16), bm=plan["out_bm"])
        h = _mlp(h, row(norm_mlp[l]), w_up[l].astype(BF16), w_down[l].astype(BF16), row(final_norm),
                 bm=plan["mlp_bm"], bf=mlp_bf, final=(l == depth - 1))

    def unpack(rows, x):
        b, s, _ = x.shape
        return rows.reshape(b, s + N_META + META_PAD, d)[:, N_META + META_PAD:]

    n_a = lay.n_a * lay.cpa * CHUNK
    return unpack(h[:n_a], x_prompt), unpack(h[n_a:], x_sample)
```

```python
import collections
import functools

import jax
import jax.numpy as jnp
from jax import lax
from jax.experimental import pallas as pl
from jax.experimental.pallas import tpu as pltpu

N_META = 16
CHUNK = 64
META_PAD = (-N_META) % CHUNK
HEAD = 128
SUBLANES = 8
POOL_WINDOWS = (2, 4, 8, 16)
POOL_HALO = 8
EPS = 1e-6
FORGET_FLOOR = 1e-30
SAFE_LOG2_DECAY = 86.0
CLEAR_LOG2_DECAY = -1e30

VMEM_LIMIT_BYTES = 56 * 1024 * 1024

F32 = jnp.float32
BF16 = jnp.bfloat16

Layout = collections.namedtuple("Layout", "n_a cpa n_b cpb")


def _num_chunks(lay):
    return lay.n_a * lay.cpa + lay.n_b * lay.cpb


def _chunk_in_seq(cg, lay):
    na = lay.n_a * lay.cpa
    in_a = cg < na
    idx = jnp.where(in_a, lax.rem(cg, lay.cpa), lax.rem(jnp.maximum(cg - na, 0), lay.cpb))
    cps = jnp.where(in_a, lay.cpa, lay.cpb)
    return idx, cps


def _largest_divisor(n, cap):
    best = 1
    for d in range(1, n + 1):
        if n % d == 0 and d <= cap:
            best = d
    return best


def _rms(x, gain):
    ms = jnp.mean(x * x, axis=-1, keepdims=True)
    return x * lax.rsqrt(ms + EPS) * gain


def _sigmoid(x):
    return 1.0 / (1.0 + jnp.exp(-x))


def _silu(x):
    hx = 0.5 * x
    return hx * jnp.tanh(hx) + hx


def _dot(a, b):
    return jnp.dot(a, b, preferred_element_type=F32)


def _dot_nt(a, b):
    return lax.dot_general(a, b, (((1,), (1,)), ((), ())), preferred_element_type=F32)


def _dot_tn(a, b):
    return lax.dot_general(a, b, (((0,), (0,)), ((), ())), preferred_element_type=F32)


def _params(sem):
    return pltpu.CompilerParams(dimension_semantics=sem, vmem_limit_bytes=VMEM_LIMIT_BYTES)


def _norm_proj_kernel(h_ref, g_ref, w_ref, o_ref, a_scr):
    @pl.when(pl.program_id(1) == 0)
    def _():
        a_scr[...] = _rms(h_ref[...], g_ref[...]).astype(BF16)

    o_ref[...] = _dot(a_scr[...], w_ref[...])


def _norm_proj(h, gain, w, *, bm, bn):
    rows, d = h.shape
    n = w.shape[1]
    return pl.pallas_call(
        _norm_proj_kernel,
        grid=(rows // bm, n // bn),
        in_specs=[
            pl.BlockSpec((bm, d), lambda i, j: (i, 0)),
            pl.BlockSpec((1, d), lambda i, j: (0, 0)),
            pl.BlockSpec((d, bn), lambda i, j: (0, j)),
        ],
        out_specs=pl.BlockSpec((bm, bn), lambda i, j: (i, j)),
        out_shape=jax.ShapeDtypeStruct((rows, n), F32),
        scratch_shapes=[pltpu.VMEM((bm, d), BF16)],
        compiler_params=_params(("parallel", "arbitrary")),
        name="norm_proj",
    )(h, gain, w)


def _forget(f_pre, lb):
    span = 1.0 - lb
    w = span * _sigmoid(f_pre)
    log2_f = jnp.log2(jnp.maximum(lb + w, FORGET_FLOOR))
    return log2_f, span - w


def _tri(lower):
    r = lax.broadcasted_iota(jnp.int32, (CHUNK, CHUNK), 0)
    c = lax.broadcasted_iota(jnp.int32, (CHUNK, CHUNK), 1)
    return (r >= c) if lower else (r <= c)


def _cumsum_rows(tri_bf16, g):
    hi = g.astype(BF16)
    lo = (g - hi.astype(F32)).astype(BF16)
    return _dot(tri_bf16, hi) + _dot(tri_bf16, lo)


def _pad_row_mask(is_first):
    r = lax.broadcasted_iota(jnp.int32, (CHUNK, 1), 0)
    return jnp.logical_or(jnp.logical_not(is_first), r >= META_PAD)


def _bwd_state_kernel(fb_ref, ip_ref, lb_ref, sb_ref, st_scr, *, lay, cpt, heads):
    tile = pl.num_programs(0) - 1 - pl.program_id(0)
    upper = jnp.where(_tri(False), 1.0, 0.0).astype(BF16)
    lb = lb_ref[...]

    @pl.when(pl.program_id(0) == 0)
    def _():
        st_scr[...] = jnp.zeros_like(st_scr)

    for c in reversed(range(cpt)):
        idx, _ = _chunk_in_seq(tile * cpt + c, lay)
        is_first = idx == 0
        st_old = st_scr[...]
        sb_ref[c] = st_old.astype(BF16)

        rows = slice(c * CHUNK, (c + 1) * CHUNK)
        v16 = ip_ref[rows, :].astype(BF16)
        g, k = _forget(fb_ref[rows, :], lb)
        cb = _cumsum_rows(upper, g)
        c_end = jnp.where(is_first, CLEAR_LOG2_DECAY, cb[0:1, :])
        khat = (k * jnp.exp2(c_end - cb)).astype(BF16)
        decayed = st_old * jnp.exp2(c_end)
        for h in range(heads):
            hs = slice(h * HEAD, (h + 1) * HEAD)
            st_scr[:, hs] = decayed[:, hs] + _dot_tn(v16[:, hs], khat[:, hs])


def _bwd_states(u, lb_b, *, lay, cpt, hgw):
    nc = _num_chunks(lay)
    nt = nc // cpt
    tr = cpt * CHUNK
    heads = hgw // HEAD
    kern = functools.partial(_bwd_state_kernel, lay=lay, cpt=cpt, heads=heads)
    return pl.pallas_call(
        kern,
        grid=(nt,),
        in_specs=[
            pl.BlockSpec((tr, hgw), lambda j: (nt - 1 - j, 2)),
            pl.BlockSpec((tr, hgw), lambda j: (nt - 1 - j, 3)),
            pl.BlockSpec((1, hgw), lambda j: (0, 0)),
        ],
        out_specs=pl.BlockSpec((cpt, HEAD, hgw), lambda j: (nt - 1 - j, 0, 0)),
        out_shape=jax.ShapeDtypeStruct((nc, HEAD, hgw), BF16),
        scratch_shapes=[pltpu.VMEM((HEAD, hgw), F32)],
        compiler_params=_params(("arbitrary",)),
        name="hgrn2_bwd_states",
    )(u, u, lb_b)


def _chunk_gates(q_ref, ff_ref, fb_ref, ip_ref, rows, lb_f, lb_b, lower, upper):
    q = _silu(q_ref[rows, :])
    v16 = ip_ref[rows, :].astype(BF16)
    gf, kf = _forget(ff_ref[rows, :], lb_f)
    gb, kb = _forget(fb_ref[rows, :], lb_b)
    bf = _cumsum_rows(lower, gf)
    cb = _cumsum_rows(upper, gb)
    return q, v16, kf, bf, kb, cb


def _head_output(o, gate, head_norm, valid, dtype):
    ms = jnp.mean(o * o, axis=-1, keepdims=True)
    y = o * lax.rsqrt(ms + EPS) * head_norm * gate
    return jnp.where(valid, y, 0.0).astype(dtype)


def _exact_scores(q_scr, k_scr, b_scr, hs, mask):
    qh = q_scr[:, hs]
    bh = b_scr[:, hs]
    lane = lax.broadcasted_iota(jnp.int32, (CHUNK, CHUNK), 1)

    def body(sg, acc):
        group = pl.ds(pl.multiple_of(sg * SUBLANES, SUBLANES), SUBLANES)
        k8 = k_scr[group, hs]
        b8 = b_scr[group, hs]
        for j in range(SUBLANES):
            e = jnp.exp2(jnp.minimum(bh - b8[j:j + 1, :], 0.0))
            col = jnp.sum(qh * k8[j:j + 1, :] * e, axis=1, keepdims=True)
            acc = jnp.where(lane == sg * SUBLANES + j, col, acc)
        return acc

    acc = lax.fori_loop(0, CHUNK // SUBLANES, body, jnp.zeros((CHUNK, CHUNK), F32))
    return jnp.where(mask, acc, 0.0)


def _hgrn2_kernel(q_ref, ff_ref, fb_ref, ip_ref, gt_ref, sb_ref, lb_ref, hn_ref, o_ref,
                  st_scr, oi_scr, redo_ref, q_scr, kf_scr, bf_scr, kb_scr, cb_scr,
                  *, lay, cpt, heads):
    tile = pl.program_id(0)
    lower_m = _tri(True)
    upper_m = _tri(False)
    lower = jnp.where(lower_m, 1.0, 0.0).astype(BF16)
    upper = jnp.where(upper_m, 1.0, 0.0).astype(BF16)
    lb_f = lb_ref[0:1, :]
    lb_b = lb_ref[1:2, :]
    hn = hn_ref[...]
    half = CHUNK // 2

    @pl.when(tile == 0)
    def _():
        st_scr[...] = jnp.zeros_like(st_scr)

    def chunk_operands(c):
        idx, cps = _chunk_in_seq(tile * cpt + c, lay)
        rows = slice(c * CHUNK, (c + 1) * CHUNK)
        q, v16, kf, bf, kb, cb = _chunk_gates(q_ref, ff_ref, fb_ref, ip_ref, rows,
                                              lb_f, lb_b, lower, upper)
        b_last = bf[CHUNK - 1:CHUNK, :]
        c_first = cb[0:1, :]
        rf = bf[half - 1:half, :]
        rb = cb[half:half + 1, :]
        worst = jnp.min(jnp.minimum(jnp.minimum(rf, b_last - rf), jnp.minimum(rb, c_first - rb)))
        redo_ref[c] = jnp.where(worst < -SAFE_LOG2_DECAY, 1, 0)

        ef = jnp.exp2(bf - rf)
        eb = jnp.exp2(cb - rb)
        b_end = jnp.where(idx == cps - 1, CLEAR_LOG2_DECAY, b_last)
        return dict(
            valid=_pad_row_mask(idx == 0),
            v16=v16,
            qtf=(q * ef).astype(BF16),
            ktf=(kf * (1.0 / ef)).astype(BF16),
            qtb=(q * eb).astype(BF16),
            ktb=(kb * (1.0 / eb)).astype(BF16),
            qhf=(q * jnp.exp2(bf)).astype(BF16),
            qhb=(q * jnp.exp2(cb)).astype(BF16),
            khat=(kf * jnp.exp2(b_end - bf)).astype(BF16),
            decay=jnp.exp2(b_end),
            gate=_silu(gt_ref[rows, :]),
        )

    def chunk_matmuls(c, x):
        rows = slice(c * CHUNK, (c + 1) * CHUNK)
        st_old = st_scr[...]
        st16 = st_old.astype(BF16)
        sb16 = sb_ref[c]
        decayed = st_old * x["decay"]
        for h in range(heads):
            hs = slice(h * HEAD, (h + 1) * HEAD)
            af = _dot_nt(x["qtf"][:, hs], x["ktf"][:, hs])
            ab = _dot_nt(x["qtb"][:, hs], x["ktb"][:, hs])
            a = (jnp.where(lower_m, af, 0.0) + jnp.where(upper_m, ab, 0.0)).astype(BF16)
            qcat = jnp.concatenate([x["qhf"][:, hs], x["qhb"][:, hs]], axis=1)
            scat = jnp.concatenate([st16[:, hs], sb16[:, hs]], axis=1)
            o_inter = _dot_nt(qcat, scat)
            oi_scr[rows, hs] = o_inter
            o = _dot(a, x["v16"][:, hs]) + o_inter
            o_ref[rows, hs] = _head_output(o, x["gate"][:, hs], hn[:, hs], x["valid"], o_ref.dtype)
            st_scr[:, hs] = decayed[:, hs] + _dot_tn(x["v16"][:, hs], x["khat"][:, hs])

    def chunk_redo(c):
        idx, _ = _chunk_in_seq(tile * cpt + c, lay)
        rows = pl.ds(pl.multiple_of(c * CHUNK, CHUNK), CHUNK)
        valid = _pad_row_mask(idx == 0)
        q, v16, kf, bf, kb, cb = _chunk_gates(q_ref, ff_ref, fb_ref, ip_ref, rows,
                                              lb_f, lb_b, lower, upper)
        q_scr[...] = q
        kf_scr[...] = kf
        bf_scr[...] = bf
        kb_scr[...] = kb
        cb_scr[...] = cb
        gate = _silu(gt_ref[rows, :])
        for h in range(heads):
            hs = slice(h * HEAD, (h + 1) * HEAD)
            a = (_exact_scores(q_scr, kf_scr, bf_scr, hs, lower_m)
                 + _exact_scores(q_scr, kb_scr, cb_scr, hs, upper_m)).astype(BF16)
            o = _dot(a, v16[:, hs]) + oi_scr[rows, hs]
            o_ref[rows, hs] = _head_output(o, gate[:, hs], hn[:, hs], valid, o_ref.dtype)

    ahead = chunk_operands(0)
    for c in range(cpt):
        current = ahead
        if c + 1 < cpt:
            ahead = chunk_operands(c + 1)
        chunk_matmuls(c, current)

    def redo_body(c, carry):
        @pl.when(redo_ref[c] != 0)
        def _():
            chunk_redo(c)
        return carry

    lax.fori_loop(0, cpt, redo_body, 0)


def _hgrn2(u, sb, lb, head_norm, *, lay, cpt, hgw):
    nc = _num_chunks(lay)
    nt = nc // cpt
    tr = cpt * CHUNK
    heads = hgw // HEAD
    kern = functools.partial(_hgrn2_kernel, lay=lay, cpt=cpt, heads=heads)
    col = lambda part: pl.BlockSpec((tr, hgw), lambda i: (i, part))
    chunk_f32 = pltpu.VMEM((CHUNK, hgw), F32)
    return pl.pallas_call(
        kern,
        grid=(nt,),
        in_specs=[
            col(0), col(1), col(2), col(3), col(4),
            pl.BlockSpec((cpt, HEAD, hgw), lambda i: (i, 0, 0)),
            pl.BlockSpec((2, hgw), lambda i: (0, 0)),
            pl.BlockSpec((1, hgw), lambda i: (0, 0)),
        ],
        out_specs=pl.BlockSpec((tr, hgw), lambda i: (i, 0)),
        out_shape=jax.ShapeDtypeStruct((nc * CHUNK, hgw), BF16),
        scratch_shapes=[
            pltpu.VMEM((HEAD, hgw), F32),
            pltpu.VMEM((tr, hgw), F32),
            pltpu.SMEM((cpt,), jnp.int32),
            chunk_f32, chunk_f32, chunk_f32, chunk_f32, chunk_f32,
        ],
        compiler_params=_params(("arbitrary",)),
        name="hgrn2_scan",
    )(u, u, u, u, u, sb, lb, head_norm)


def _pool_kernel(prev_ref, x_ref, next_ref, wp_ref, ps_ref, o_ref, xs_scr, *, lay, tm):
    idx, cps = _chunk_in_seq(pl.program_id(0) * (tm // CHUNK), lay)
    p0 = idx * CHUNK
    seq_rows = cps * CHUNK
    pos_h = p0 - POOL_HALO + lax.broadcasted_iota(jnp.int32, (tm + 2 * POOL_HALO, 1), 0)
    valid_h = jnp.logical_and(pos_h >= META_PAD, pos_h < seq_rows)
    x_all = jnp.concatenate([prev_ref[...], x_ref[...], next_ref[...]], axis=0)
    xs_scr[...] = jnp.where(valid_h, x_all, 0.0)

    pos = p0 + lax.broadcasted_iota(jnp.int32, (tm, 1), 0)
    valid = pos >= META_PAD
    gw = x_ref.shape[1] // len(POOL_WINDOWS)
    for g, window in enumerate(POOL_WINDOWS):
        cols = slice(g * gw, (g + 1) * gw)
        back = window // 2
        total = xs_scr[pl.ds(POOL_HALO - back, tm), cols]
        for j in range(1, window):
            total = total + xs_scr[pl.ds(POOL_HALO - back + j, tm), cols]
        count = (jnp.minimum(pos + (window - back), seq_rows) - jnp.maximum(pos - back, META_PAD))
        count = jnp.maximum(count, 1).astype(F32)
        pooled = total / count - xs_scr[pl.ds(POOL_HALO, tm), cols]
        y = _dot(pooled.astype(BF16), wp_ref[g]) * ps_ref[:, cols]
        o_ref[:, cols] = jnp.where(valid, y, 0.0).astype(o_ref.dtype)


def _pool(u, w_pool, pool_scale, *, lay, tm, hgw):
    rows = u.shape[0]
    pw = pool_scale.shape[1]
    col = 5 * hgw // pw
    hb = tm // POOL_HALO
    last_hb = rows // POOL_HALO - 1
    kern = functools.partial(_pool_kernel, lay=lay, tm=tm)
    return pl.pallas_call(
        kern,
        grid=(rows // tm,),
        in_specs=[
            pl.BlockSpec((POOL_HALO, pw), lambda i: (jnp.maximum(i * hb - 1, 0), col)),
            pl.BlockSpec((tm, pw), lambda i: (i, col)),
            pl.BlockSpec((POOL_HALO, pw), lambda i: (jnp.minimum((i + 1) * hb, last_hb), col)),
            pl.BlockSpec(w_pool.shape, lambda i: (0, 0, 0)),
            pl.BlockSpec((1, pw), lambda i: (0, 0)),
        ],
        out_specs=pl.BlockSpec((tm, pw), lambda i: (i, 0)),
        out_shape=jax.ShapeDtypeStruct((rows, pw), BF16),
        scratch_shapes=[pltpu.VMEM((tm + 2 * POOL_HALO, pw), F32)],
        compiler_params=_params(("parallel",)),
        name="pool_mixer",
    )(u, u, u, w_pool, pool_scale)


def _out_proj_kernel(h_ref, yh_ref, yp_ref, w_ref, o_ref):
    hgw = yh_ref.shape[1]
    o_ref[...] = (h_ref[...] + _dot(yh_ref[...], w_ref[0:hgw, :])
                  + _dot(yp_ref[...], w_ref[hgw:, :]))


def _out_proj(h, y_hg, y_pool, w_out, *, bm):
    rows, d = h.shape
    hgw, pw = y_hg.shape[1], y_pool.shape[1]
    return pl.pallas_call(
        _out_proj_kernel,
        grid=(rows // bm,),
        in_specs=[
            pl.BlockSpec((bm, d), lambda i: (i, 0)),
            pl.BlockSpec((bm, hgw), lambda i: (i, 0)),
            pl.BlockSpec((bm, pw), lambda i: (i, 0)),
            pl.BlockSpec(w_out.shape, lambda i: (0, 0)),
        ],
        out_specs=pl.BlockSpec((bm, d), lambda i: (i, 0)),
        out_shape=jax.ShapeDtypeStruct((rows, d), F32),
        compiler_params=_params(("parallel",)),
        name="out_proj",
    )(h, y_hg, y_pool, w_out)


def _mlp_kernel(h_ref, g_ref, wu_ref, wd_ref, fg_ref, o_ref, m_scr, *, final, f_axis):
    f = pl.program_id(f_axis)

    @pl.when(f == 0)
    def _():
        h = h_ref[...]
        m_scr[...] = _rms(h, g_ref[...]).astype(BF16)
        o_ref[...] = h

    hidden = jnp.square(jnp.maximum(_dot(m_scr[...], wu_ref[...]), 0.0)).astype(BF16)
    o_ref[...] += _dot(hidden, wd_ref[...])

    if final:
        @pl.when(f == pl.num_programs(f_axis) - 1)
        def _():
            o_ref[...] = _rms(o_ref[...], fg_ref[...])


def _mlp(h, gain, w_up, w_down, *, bm, bf):
    rows, d = h.shape
    dff = w_up.shape[1]
    return pl.pallas_call(
        functools.partial(_mlp_kernel, final=False, f_axis=1),
        grid=(rows // bm, dff // bf),
        in_specs=[
            pl.BlockSpec((bm, d), lambda i, f: (i, 0)),
            pl.BlockSpec((1, d), lambda i, f: (0, 0)),
            pl.BlockSpec((d, bf), lambda i, f: (0, f)),
            pl.BlockSpec((bf, d), lambda i, f: (f, 0)),
            pl.BlockSpec((1, d), lambda i, f: (0, 0)),
        ],
        out_specs=pl.BlockSpec((bm, d), lambda i, f: (i, 0)),
        out_shape=jax.ShapeDtypeStruct((rows, d), F32),
        scratch_shapes=[pltpu.VMEM((bm, d), BF16)],
        compiler_params=_params(("parallel", "arbitrary")),
        name="mlp",
    )(h, gain, w_up, w_down, gain)


def _mlp_final(h, gain, w_up, w_down, final_gain, *, row0, n_seq, seq_rows, s, bm, bf):
    d = h.shape[1]
    dff = w_up.shape[1]
    skip = seq_rows - s
    return pl.pallas_call(
        functools.partial(_mlp_kernel, final=True, f_axis=2),
        grid=(n_seq, s // bm, dff // bf),
        in_specs=[
            pl.BlockSpec((pl.Element(bm), pl.Element(d)),
                         lambda b, i, f: (pl.multiple_of(row0 + b * seq_rows + skip + i * bm, CHUNK), 0)),
            pl.BlockSpec((1, d), lambda b, i, f: (0, 0)),
            pl.BlockSpec((d, bf), lambda b, i, f: (0, f)),
            pl.BlockSpec((bf, d), lambda b, i, f: (f, 0)),
            pl.BlockSpec((1, d), lambda b, i, f: (0, 0)),
        ],
        out_specs=pl.BlockSpec((None, bm, d), lambda b, i, f: (b, i, 0)),
        out_shape=jax.ShapeDtypeStruct((n_seq, s, d), F32),
        scratch_shapes=[pltpu.VMEM((bm, d), BF16)],
        compiler_params=_params(("parallel", "parallel", "arbitrary")),
        name="mlp_final",
    )(h, gain, w_up, w_down, final_gain)


def _pack_rows(x, meta):
    b, _, d = x.shape
    pad = jnp.zeros((b, META_PAD, d), x.dtype)
    m = jnp.broadcast_to(meta.astype(x.dtype)[None], (b, N_META, d))
    return jnp.concatenate([pad, m, x], axis=1).reshape(-1, d)


def _tile_plan(lay):
    nc = _num_chunks(lay)
    g = 1
    for d in range(1, min(lay.cpa, lay.cpb) + 1):
        if lay.cpa % d == 0 and lay.cpb % d == 0:
            g = d
    return dict(
        proj_bm=CHUNK * _largest_divisor(nc, 12),
        scan_cpt=_largest_divisor(nc, 6),
        pool_tm=CHUNK * _largest_divisor(g, 4),
        out_bm=CHUNK * _largest_divisor(nc, 10),
        mlp_bm=CHUNK * _largest_divisor(nc, 10),
    )


def kernel(x_prompt, x_sample, meta_tokens, w_in, w_pool, pool_scale, hg_lower_bound, hg_head_norm,
           w_out, norm_mix, norm_mlp, w_up, w_down, final_norm):
    depth, d, in_cols = w_in.shape
    pw = pool_scale.shape[1]
    hgw = hg_head_norm.shape[1]
    dff = w_up.shape[2]
    lead = N_META + META_PAD
    s_a, s_b = x_prompt.shape[1], x_sample.shape[1]
    assert in_cols == 5 * hgw + pw and hgw % HEAD == 0 and pw == hgw
    assert s_a % CHUNK == 0 and s_b % CHUNK == 0
    lay = Layout(x_prompt.shape[0], (s_a + lead) // CHUNK, x_sample.shape[0], (s_b + lead) // CHUNK)
    plan = _tile_plan(lay)
    proj_bn = in_cols // _largest_divisor(in_cols // 256, 4)
    mlp_bf = dff // _largest_divisor(dff // 256, 8)

    probs = jax.nn.softmax(hg_lower_bound.astype(F32), axis=1)
    lower = jnp.cumsum(probs, axis=1) - probs[:, :1]

    h = jnp.concatenate([_pack_rows(x_prompt, meta_tokens), _pack_rows(x_sample, meta_tokens)], axis=0)
    row = lambda a: a.astype(F32).reshape(1, -1)
    for l in range(depth):
        u = _norm_proj(h, row(norm_mix[l]), w_in[l].astype(BF16), bm=plan["proj_bm"], bn=proj_bn)
        sb = _bwd_states(u, lower[1:2, l], lay=lay, cpt=plan["scan_cpt"], hgw=hgw)
        y_hg = _hgrn2(u, sb, lower[:, l], row(hg_head_norm[l]), lay=lay, cpt=plan["scan_cpt"], hgw=hgw)
        y_pool = _pool(u, w_pool[l].astype(BF16), row(pool_scale[l]), lay=lay, tm=plan["pool_tm"], hgw=hgw)
        h = _out_proj(h, y_hg, y_pool, w_out[l].astype(BF16), bm=plan["out_bm"])
        mlp_w = (row(norm_mlp[l]), w_up[l].astype(BF16), w_down[l].astype(BF16))
        if l < depth - 1:
            h = _mlp(h, *mlp_w, bm=plan["mlp_bm"], bf=mlp_bf)

    def final(row0, n_seq, s):
        bm = CHUNK * _largest_divisor(s // CHUNK, 8)
        return _mlp_final(h, *mlp_w, row(final_norm), row0=row0, n_seq=n_seq, seq_rows=s + lead,
                          s=s, bm=bm, bf=mlp_bf)

    return (final(0, lay.n_a, s_a), final(lay.n_a * lay.cpa * CHUNK, lay.n_b, s_b))
```

```python
import collections
import functools

import jax
import jax.numpy as jnp
from jax import lax
from jax.experimental import pallas as pl
from jax.experimental.pallas import tpu as pltpu

N_META = 16
CHUNK = 64
META_PAD = (-N_META) % CHUNK
HEAD = 128
SUBLANES = 8
POOL_WINDOWS = (2, 4, 8, 16)
POOL_HALO = 8
EPS = 1e-6
FORGET_FLOOR = 1e-30
SAFE_LOG2_DECAY = 86.0
CLEAR_LOG2_DECAY = -1e30

VMEM_LIMIT_BYTES = 56 * 1024 * 1024

F32 = jnp.float32
BF16 = jnp.bfloat16

Layout = collections.namedtuple("Layout", "n_a cpa n_b cpb")


def _num_chunks(lay):
    return lay.n_a * lay.cpa + lay.n_b * lay.cpb


def _chunk_in_seq(cg, lay):
    na = lay.n_a * lay.cpa
    in_a = cg < na
    idx = jnp.where(in_a, lax.rem(cg, lay.cpa), lax.rem(jnp.maximum(cg - na, 0), lay.cpb))
    cps = jnp.where(in_a, lay.cpa, lay.cpb)
    return idx, cps


def _largest_divisor(n, cap):
    best = 1
    for d in range(1, n + 1):
        if n % d == 0 and d <= cap:
            best = d
    return best


def _rms(x, gain):
    ms = jnp.mean(x * x, axis=-1, keepdims=True)
    return x * lax.rsqrt(ms + EPS) * gain


def _sigmoid(x):
    return 1.0 / (1.0 + jnp.exp(-x))


def _silu(x):
    hx = 0.5 * x
    return hx * jnp.tanh(hx) + hx


def _dot(a, b):
    return jnp.dot(a, b, preferred_element_type=F32)


def _dot_nt(a, b):
    return lax.dot_general(a, b, (((1,), (1,)), ((), ())), preferred_element_type=F32)


def _dot_tn(a, b):
    return lax.dot_general(a, b, (((0,), (0,)), ((), ())), preferred_element_type=F32)


def _params(sem):
    return pltpu.CompilerParams(dimension_semantics=sem, vmem_limit_bytes=VMEM_LIMIT_BYTES)


def _norm_proj_kernel(h_ref, g_ref, w_ref, o_ref, a_scr):
    @pl.when(pl.program_id(1) == 0)
    def _():
        a_scr[...] = _rms(h_ref[...], g_ref[...]).astype(BF16)

    o_ref[...] = _dot(a_scr[...], w_ref[...])


def _norm_proj(h, gain, w, *, bm, bn):
    rows, d = h.shape
    n = w.shape[1]
    return pl.pallas_call(
        _norm_proj_kernel,
        grid=(rows // bm, n // bn),
        in_specs=[
            pl.BlockSpec((bm, d), lambda i, j: (i, 0)),
            pl.BlockSpec((1, d), lambda i, j: (0, 0)),
            pl.BlockSpec((d, bn), lambda i, j: (0, j)),
        ],
        out_specs=pl.BlockSpec((bm, bn), lambda i, j: (i, j)),
        out_shape=jax.ShapeDtypeStruct((rows, n), F32),
        scratch_shapes=[pltpu.VMEM((bm, d), BF16)],
        compiler_params=_params(("parallel", "arbitrary")),
        name="norm_proj",
    )(h, gain, w)


def _forget(f_pre, lb):
    span = 1.0 - lb
    w = span * _sigmoid(f_pre)
    log2_f = jnp.log2(jnp.maximum(lb + w, FORGET_FLOOR))
    return log2_f, span - w


def _tri(lower):
    r = lax.broadcasted_iota(jnp.int32, (CHUNK, CHUNK), 0)
    c = lax.broadcasted_iota(jnp.int32, (CHUNK, CHUNK), 1)
    return (r >= c) if lower else (r <= c)


def _cumsum_rows(tri_bf16, g):
    hi = g.astype(BF16)
    lo = (g - hi.astype(F32)).astype(BF16)
    return _dot(tri_bf16, hi) + _dot(tri_bf16, lo)


def _pad_row_mask(is_first):
    r = lax.broadcasted_iota(jnp.int32, (CHUNK, 1), 0)
    return jnp.logical_or(jnp.logical_not(is_first), r >= META_PAD)


def _bwd_state_kernel(fb_ref, ip_ref, lb_ref, sb_ref, st_scr, *, lay, cpt, heads):
    tile = pl.num_programs(0) - 1 - pl.program_id(0)
    upper = jnp.where(_tri(False), 1.0, 0.0).astype(BF16)
    lb = lb_ref[...]

    @pl.when(pl.program_id(0) == 0)
    def _():
        st_scr[...] = jnp.zeros_like(st_scr)

    group = 2 if heads % 2 == 0 else 1
    slabs = [(c, g) for c in reversed(range(cpt)) for g in range(heads // group)]

    def stage_gates(c, g):
        idx, _ = _chunk_in_seq(tile * cpt + c, lay)
        rows = slice(c * CHUNK, (c + 1) * CHUNK)
        cols = slice(g * group * HEAD, (g + 1) * group * HEAD)
        log2_f, k = _forget(fb_ref[rows, cols], lb[:, cols])
        return dict(c=c, g=g, is_first=idx == 0, k=k,
                    v16=ip_ref[rows, cols].astype(BF16),
                    cb=_cumsum_rows(upper, log2_f))

    def stage_operands(x):
        c_end = jnp.where(x["is_first"], CLEAR_LOG2_DECAY, x["cb"][0:1, :])
        x["khat"] = (x["k"] * jnp.exp2(c_end - x["cb"])).astype(BF16)
        x["decay"] = jnp.exp2(c_end)

    def stage_update(x):
        for j in range(group):
            h = x["g"] * group + j
            hs = slice(h * HEAD, (h + 1) * HEAD)
            hl = slice(j * HEAD, (j + 1) * HEAD)
            st_old = st_scr[:, hs]
            sb_ref[x["c"], :, hs] = st_old.astype(BF16)
            st_scr[:, hs] = st_old * x["decay"][:, hl] + _dot_tn(x["v16"][:, hl], x["khat"][:, hl])

    stages = (stage_operands, stage_update)
    in_flight = []
    for step in range(len(slabs) + len(stages)):
        in_flight.insert(0, stage_gates(*slabs[step]) if step < len(slabs) else None)
        in_flight = in_flight[:len(stages) + 1]
        for depth, stage in enumerate(stages, start=1):
            if depth < len(in_flight) and in_flight[depth] is not None:
                stage(in_flight[depth])


def _bwd_states(u, lb_b, *, lay, cpt, hgw):
    nc = _num_chunks(lay)
    nt = nc // cpt
    tr = cpt * CHUNK
    heads = hgw // HEAD
    kern = functools.partial(_bwd_state_kernel, lay=lay, cpt=cpt, heads=heads)
    return pl.pallas_call(
        kern,
        grid=(nt,),
        in_specs=[
            pl.BlockSpec((tr, hgw), lambda j: (nt - 1 - j, 2)),
            pl.BlockSpec((tr, hgw), lambda j: (nt - 1 - j, 3)),
            pl.BlockSpec((1, hgw), lambda j: (0, 0)),
        ],
        out_specs=pl.BlockSpec((cpt, HEAD, hgw), lambda j: (nt - 1 - j, 0, 0)),
        out_shape=jax.ShapeDtypeStruct((nc, HEAD, hgw), BF16),
        scratch_shapes=[pltpu.VMEM((HEAD, hgw), F32)],
        compiler_params=_params(("arbitrary",)),
        name="hgrn2_bwd_states",
    )(u, u, lb_b)


def _chunk_gates(q_ref, ff_ref, fb_ref, ip_ref, rows, cols, lb_f, lb_b, lower, upper):
    q = _silu(q_ref[rows, cols])
    v16 = ip_ref[rows, cols].astype(BF16)
    gf, kf = _forget(ff_ref[rows, cols], lb_f[:, cols])
    gb, kb = _forget(fb_ref[rows, cols], lb_b[:, cols])
    bf = _cumsum_rows(lower, gf)
    cb = _cumsum_rows(upper, gb)
    return q, v16, kf, bf, kb, cb


def _head_output(o, gate, head_norm, valid, dtype):
    ms = jnp.mean(o * o, axis=-1, keepdims=True)
    y = o * lax.rsqrt(ms + EPS) * head_norm * gate
    return jnp.where(valid, y, 0.0).astype(dtype)


def _exact_scores(q_scr, k_scr, b_scr, hs, mask):
    qh = q_scr[:, hs]
    bh = b_scr[:, hs]
    lane = lax.broadcasted_iota(jnp.int32, (CHUNK, CHUNK), 1)

    def body(sg, acc):
        group = pl.ds(pl.multiple_of(sg * SUBLANES, SUBLANES), SUBLANES)
        k8 = k_scr[group, hs]
        b8 = b_scr[group, hs]
        for j in range(SUBLANES):
            e = jnp.exp2(jnp.minimum(bh - b8[j:j + 1, :], 0.0))
            col = jnp.sum(qh * k8[j:j + 1, :] * e, axis=1, keepdims=True)
            acc = jnp.where(lane == sg * SUBLANES + j, col, acc)
        return acc

    acc = lax.fori_loop(0, CHUNK // SUBLANES, body, jnp.zeros((CHUNK, CHUNK), F32))
    return jnp.where(mask, acc, 0.0)


def _hgrn2_kernel(q_ref, ff_ref, fb_ref, ip_ref, gt_ref, sb_ref, lb_ref, hn_ref, o_ref,
                  st_scr, oi_scr, redo_ref, q_scr, kf_scr, bf_scr, kb_scr, cb_scr,
                  *, lay, cpt, heads):
    tile = pl.program_id(0)
    lower_m = _tri(True)
    upper_m = _tri(False)
    lower = jnp.where(lower_m, 1.0, 0.0).astype(BF16)
    upper = jnp.where(upper_m, 1.0, 0.0).astype(BF16)
    lb_f = lb_ref[0:1, :]
    lb_b = lb_ref[1:2, :]
    hn = hn_ref[...]
    half = CHUNK // 2

    @pl.when(tile == 0)
    def _():
        st_scr[...] = jnp.zeros_like(st_scr)

    group = 2 if heads % 2 == 0 else 1
    slabs = [(c, g) for c in range(cpt) for g in range(heads // group)]
    margin = {}

    def stage_gates(c, g):
        idx, cps = _chunk_in_seq(tile * cpt + c, lay)
        rows = slice(c * CHUNK, (c + 1) * CHUNK)
        cols = slice(g * group * HEAD, (g + 1) * group * HEAD)
        q, v16, kf, bf, kb, cb = _chunk_gates(q_ref, ff_ref, fb_ref, ip_ref, rows, cols,
                                              lb_f, lb_b, lower, upper)
        return dict(c=c, g=g, rows=rows, valid=_pad_row_mask(idx == 0), is_last=idx == cps - 1,
                    q=q, v16=v16, kf=kf, bf=bf, kb=kb, cb=cb, gate=_silu(gt_ref[rows, cols]))

    def stage_operands(x):
        q, kf, bf, kb, cb = x["q"], x["kf"], x["bf"], x["kb"], x["cb"]
        b_last = bf[CHUNK - 1:CHUNK, :]
        c_first = cb[0:1, :]
        rf = bf[half - 1:half, :]
        rb = cb[half:half + 1, :]
        ef = jnp.exp2(bf - rf)
        eb = jnp.exp2(cb - rb)
        b_end = jnp.where(x["is_last"], CLEAR_LOG2_DECAY, b_last)
        m = jnp.minimum(jnp.minimum(rf, b_last - rf), jnp.minimum(rb, c_first - rb))
        margin[x["c"]] = m if x["g"] == 0 else jnp.minimum(margin[x["c"]], m)
        if x["g"] == heads // group - 1:
            redo_ref[x["c"]] = jnp.where(jnp.min(margin[x["c"]]) < -SAFE_LOG2_DECAY, 1, 0)
        x.update(
            qtf=(q * ef).astype(BF16),
            ktf=(kf * (1.0 / ef)).astype(BF16),
            qtb=(q * eb).astype(BF16),
            ktb=(kb * (1.0 / eb)).astype(BF16),
            qhf=(q * jnp.exp2(bf)).astype(BF16),
            qhb=(q * jnp.exp2(cb)).astype(BF16),
            khat=(kf * jnp.exp2(b_end - bf)).astype(BF16),
            decay=jnp.exp2(b_end),
        )

    def group_heads(x):
        for j in range(group):
            h = x["g"] * group + j
            yield j, slice(h * HEAD, (h + 1) * HEAD), slice(j * HEAD, (j + 1) * HEAD)

    def stage_scores(x):
        x["af"], x["ab"], x["oi"] = {}, {}, {}
        for j, hs, hl in group_heads(x):
            st_old = st_scr[:, hs]
            x["af"][j] = _dot_nt(x["qtf"][:, hl], x["ktf"][:, hl])
            x["ab"][j] = _dot_nt(x["qtb"][:, hl], x["ktb"][:, hl])
            qcat = jnp.concatenate([x["qhf"][:, hl], x["qhb"][:, hl]], axis=1)
            scat = jnp.concatenate([st_old.astype(BF16), sb_ref[x["c"], :, hs]], axis=1)
            x["oi"][j] = _dot_nt(qcat, scat)
            st_scr[:, hs] = st_old * x["decay"][:, hl] + _dot_tn(x["v16"][:, hl], x["khat"][:, hl])

    def stage_mix(x):
        x["o"] = {}
        for j, hs, hl in group_heads(x):
            a = (jnp.where(lower_m, x["af"][j], 0.0) + jnp.where(upper_m, x["ab"][j], 0.0)).astype(BF16)
            oi_scr[x["rows"], hs] = x["oi"][j]
            x["o"][j] = _dot(a, x["v16"][:, hl]) + x["oi"][j]

    def stage_output(x):
        for j, hs, hl in group_heads(x):
            o_ref[x["rows"], hs] = _head_output(x["o"][j], x["gate"][:, hl], hn[:, hs], x["valid"],
                                                o_ref.dtype)

    def chunk_redo(c):
        idx, _ = _chunk_in_seq(tile * cpt + c, lay)
        rows = pl.ds(pl.multiple_of(c * CHUNK, CHUNK), CHUNK)
        valid = _pad_row_mask(idx == 0)
        q, v16, kf, bf, kb, cb = _chunk_gates(q_ref, ff_ref, fb_ref, ip_ref, rows, slice(None),
                                              lb_f, lb_b, lower, upper)
        q_scr[...] = q
        kf_scr[...] = kf
        bf_scr[...] = bf
        kb_scr[...] = kb
        cb_scr[...] = cb
        gate = _silu(gt_ref[rows, :])
        for h in range(heads):
            hs = slice(h * HEAD, (h + 1) * HEAD)
            a = (_exact_scores(q_scr, kf_scr, bf_scr, hs, lower_m)
                 + _exact_scores(q_scr, kb_scr, cb_scr, hs, upper_m)).astype(BF16)
            o = _dot(a, v16[:, hs]) + oi_scr[rows, hs]
            o_ref[rows, hs] = _head_output(o, gate[:, hs], hn[:, hs], valid, o_ref.dtype)

    stages = (stage_operands, stage_scores, stage_mix, stage_output)
    in_flight = []
    for step in range(len(slabs) + len(stages)):
        in_flight.insert(0, stage_gates(*slabs[step]) if step < len(slabs) else None)
        in_flight = in_flight[:len(stages) + 1]
        for depth, stage in enumerate(stages, start=1):
            if depth < len(in_flight) and in_flight[depth] is not None:
                stage(in_flight[depth])

    def redo_body(c, carry):
        @pl.when(redo_ref[c] != 0)
        def _():
            chunk_redo(c)
        return carry

    lax.fori_loop(0, cpt, redo_body, 0)


def _hgrn2(u, sb, lb, head_norm, *, lay, cpt, hgw):
    nc = _num_chunks(lay)
    nt = nc // cpt
    tr = cpt * CHUNK
    heads = hgw // HEAD
    kern = functools.partial(_hgrn2_kernel, lay=lay, cpt=cpt, heads=heads)
    col = lambda part: pl.BlockSpec((tr, hgw), lambda i: (i, part))
    chunk_f32 = pltpu.VMEM((CHUNK, hgw), F32)
    return pl.pallas_call(
        kern,
        grid=(nt,),
        in_specs=[
            col(0), col(1), col(2), col(3), col(4),
            pl.BlockSpec((cpt, HEAD, hgw), lambda i: (i, 0, 0)),
            pl.BlockSpec((2, hgw), lambda i: (0, 0)),
            pl.BlockSpec((1, hgw), lambda i: (0, 0)),
        ],
        out_specs=pl.BlockSpec((tr, hgw), lambda i: (i, 0)),
        out_shape=jax.ShapeDtypeStruct((nc * CHUNK, hgw), BF16),
        scratch_shapes=[
            pltpu.VMEM((HEAD, hgw), F32),
            pltpu.VMEM((tr, hgw), F32),
            pltpu.SMEM((cpt,), jnp.int32),
            chunk_f32, chunk_f32, chunk_f32, chunk_f32, chunk_f32,
        ],
        compiler_params=_params(("arbitrary",)),
        name="hgrn2_scan",
    )(u, u, u, u, u, sb, lb, head_norm)


def _pool_kernel(prev_ref, x_ref, next_ref, wp_ref, ps_ref, o_ref, xs_scr, *, lay, tm):
    idx, cps = _chunk_in_seq(pl.program_id(0) * (tm // CHUNK), lay)
    p0 = idx * CHUNK
    seq_rows = cps * CHUNK
    pos_h = p0 - POOL_HALO + lax.broadcasted_iota(jnp.int32, (tm + 2 * POOL_HALO, 1), 0)
    valid_h = jnp.logical_and(pos_h >= META_PAD, pos_h < seq_rows)
    x_all = jnp.concatenate([prev_ref[...], x_ref[...], next_ref[...]], axis=0)
    xs_scr[...] = jnp.where(valid_h, x_all, 0.0)

    pos = p0 + lax.broadcasted_iota(jnp.int32, (tm, 1), 0)
    valid = pos >= META_PAD
    gw = x_ref.shape[1] // len(POOL_WINDOWS)
    for g, window in enumerate(POOL_WINDOWS):
        cols = slice(g * gw, (g + 1) * gw)
        back = window // 2
        total = xs_scr[pl.ds(POOL_HALO - back, tm), cols]
        for j in range(1, window):
            total = total + xs_scr[pl.ds(POOL_HALO - back + j, tm), cols]
        count = (jnp.minimum(pos + (window - back), seq_rows) - jnp.maximum(pos - back, META_PAD))
        count = jnp.maximum(count, 1).astype(F32)
        pooled = total / count - xs_scr[pl.ds(POOL_HALO, tm), cols]
        y = _dot(pooled.astype(BF16), wp_ref[g]) * ps_ref[:, cols]
        o_ref[:, cols] = jnp.where(valid, y, 0.0).astype(o_ref.dtype)


def _pool(u, w_pool, pool_scale, *, lay, tm, hgw):
    rows = u.shape[0]
    pw = pool_scale.shape[1]
    col = 5 * hgw // pw
    hb = tm // POOL_HALO
    last_hb = rows // POOL_HALO - 1
    kern = functools.partial(_pool_kernel, lay=lay, tm=tm)
    return pl.pallas_call(
        kern,
        grid=(rows // tm,),
        in_specs=[
            pl.BlockSpec((POOL_HALO, pw), lambda i: (jnp.maximum(i * hb - 1, 0), col)),
            pl.BlockSpec((tm, pw), lambda i: (i, col)),
            pl.BlockSpec((POOL_HALO, pw), lambda i: (jnp.minimum((i + 1) * hb, last_hb), col)),
            pl.BlockSpec(w_pool.shape, lambda i: (0, 0, 0)),
            pl.BlockSpec((1, pw), lambda i: (0, 0)),
        ],
        out_specs=pl.BlockSpec((tm, pw), lambda i: (i, 0)),
        out_shape=jax.ShapeDtypeStruct((rows, pw), BF16),
        scratch_shapes=[pltpu.VMEM((tm + 2 * POOL_HALO, pw), F32)],
        compiler_params=_params(("parallel",)),
        name="pool_mixer",
    )(u, u, u, w_pool, pool_scale)


def _out_proj_kernel(h_ref, yh_ref, yp_ref, w_ref, o_ref):
    hgw = yh_ref.shape[1]
    o_ref[...] = (h_ref[...] + _dot(yh_ref[...], w_ref[0:hgw, :])
                  + _dot(yp_ref[...], w_ref[hgw:, :]))


def _out_proj(h, y_hg, y_pool, w_out, *, bm):
    rows, d = h.shape
    hgw, pw = y_hg.shape[1], y_pool.shape[1]
    return pl.pallas_call(
        _out_proj_kernel,
        grid=(rows // bm,),
        in_specs=[
            pl.BlockSpec((bm, d), lambda i: (i, 0)),
            pl.BlockSpec((bm, hgw), lambda i: (i, 0)),
            pl.BlockSpec((bm, pw), lambda i: (i, 0)),
            pl.BlockSpec(w_out.shape, lambda i: (0, 0)),
        ],
        out_specs=pl.BlockSpec((bm, d), lambda i: (i, 0)),
        out_shape=jax.ShapeDtypeStruct((rows, d), F32),
        compiler_params=_params(("parallel",)),
        name="out_proj",
    )(h, y_hg, y_pool, w_out)


def _mlp_kernel(h_ref, g_ref, wu_ref, wd_ref, fg_ref, o_ref, m_scr, *, final, f_axis):
    f = pl.program_id(f_axis)

    @pl.when(f == 0)
    def _():
        h = h_ref[...]
        m_scr[...] = _rms(h, g_ref[...]).astype(BF16)
        o_ref[...] = h

    hidden = jnp.square(jnp.maximum(_dot(m_scr[...], wu_ref[...]), 0.0)).astype(BF16)
    o_ref[...] += _dot(hidden, wd_ref[...])

    if final:
        @pl.when(f == pl.num_programs(f_axis) - 1)
        def _():
            o_ref[...] = _rms(o_ref[...], fg_ref[...])


def _mlp(h, gain, w_up, w_down, *, bm, bf):
    rows, d = h.shape
    dff = w_up.shape[1]
    return pl.pallas_call(
        functools.partial(_mlp_kernel, final=False, f_axis=1),
        grid=(rows // bm, dff // bf),
        in_specs=[
            pl.BlockSpec((bm, d), lambda i, f: (i, 0)),
            pl.BlockSpec((1, d), lambda i, f: (0, 0)),
            pl.BlockSpec((d, bf), lambda i, f: (0, f)),
            pl.BlockSpec((bf, d), lambda i, f: (f, 0)),
            pl.BlockSpec((1, d), lambda i, f: (0, 0)),
        ],
        out_specs=pl.BlockSpec((bm, d), lambda i, f: (i, 0)),
        out_shape=jax.ShapeDtypeStruct((rows, d), F32),
        scratch_shapes=[pltpu.VMEM((bm, d), BF16)],
        compiler_params=_params(("parallel", "arbitrary")),
        name="mlp",
    )(h, gain, w_up, w_down, gain)


def _mlp_final(h, gain, w_up, w_down, final_gain, *, row0, n_seq, seq_rows, s, bm, bf):
    d = h.shape[1]
    dff = w_up.shape[1]
    skip = seq_rows - s
    return pl.pallas_call(
        functools.partial(_mlp_kernel, final=True, f_axis=2),
        grid=(n_seq, s // bm, dff // bf),
        in_specs=[
            pl.BlockSpec((pl.Element(bm), pl.Element(d)),
                         lambda b, i, f: (pl.multiple_of(row0 + b * seq_rows + skip + i * bm, CHUNK), 0)),
            pl.BlockSpec((1, d), lambda b, i, f: (0, 0)),
            pl.BlockSpec((d, bf), lambda b, i, f: (0, f)),
            pl.BlockSpec((bf, d), lambda b, i, f: (f, 0)),
            pl.BlockSpec((1, d), lambda b, i, f: (0, 0)),
        ],
        out_specs=pl.BlockSpec((None, bm, d), lambda b, i, f: (b, i, 0)),
        out_shape=jax.ShapeDtypeStruct((n_seq, s, d), F32),
        scratch_shapes=[pltpu.VMEM((bm, d), BF16)],
        compiler_params=_params(("parallel", "parallel", "arbitrary")),
        name="mlp_final",
    )(h, gain, w_up, w_down, final_gain)


def _pack_rows(xs, meta):
    d = meta.shape[1]
    lead = jnp.concatenate([jnp.zeros((META_PAD, d), F32), meta.astype(F32)], axis=0)
    pieces = []
    for x in xs:
        for b in range(x.shape[0]):
            pieces += [lead, x[b]]
    return jnp.concatenate(pieces, axis=0)


def _tile_plan(lay):
    nc = _num_chunks(lay)
    g = 1
    for d in range(1, min(lay.cpa, lay.cpb) + 1):
        if lay.cpa % d == 0 and lay.cpb % d == 0:
            g = d
    return dict(
        proj_bm=CHUNK * _largest_divisor(nc, 12),
        scan_cpt=_largest_divisor(nc, 12),
        pool_tm=CHUNK * _largest_divisor(g, 4),
        out_bm=CHUNK * _largest_divisor(nc, 10),
        mlp_bm=CHUNK * _largest_divisor(nc, 10),
    )


def kernel(x_prompt, x_sample, meta_tokens, w_in, w_pool, pool_scale, hg_lower_bound, hg_head_norm,
           w_out, norm_mix, norm_mlp, w_up, w_down, final_norm):
    depth, d, in_cols = w_in.shape
    pw = pool_scale.shape[1]
    hgw = hg_head_norm.shape[1]
    dff = w_up.shape[2]
    lead = N_META + META_PAD
    s_a, s_b = x_prompt.shape[1], x_sample.shape[1]
    assert in_cols == 5 * hgw + pw and hgw % HEAD == 0 and pw == hgw
    assert s_a % CHUNK == 0 and s_b % CHUNK == 0
    lay = Layout(x_prompt.shape[0], (s_a + lead) // CHUNK, x_sample.shape[0], (s_b + lead) // CHUNK)
    plan = _tile_plan(lay)
    proj_bn = in_cols // _largest_divisor(in_cols // 256, 4)
    mlp_bf = dff // _largest_divisor(dff // 256, 8)

    probs = jax.nn.softmax(hg_lower_bound.astype(F32), axis=1)
    lower = jnp.cumsum(probs, axis=1) - probs[:, :1]

    h = _pack_rows((x_prompt, x_sample), meta_tokens)
    row = lambda a: a.astype(F32).reshape(1, -1)
    for l in range(depth):
        u = _norm_proj(h, row(norm_mix[l]), w_in[l].astype(BF16), bm=plan["proj_bm"], bn=proj_bn)
        sb = _bwd_states(u, lower[1:2, l], lay=lay, cpt=plan["scan_cpt"], hgw=hgw)
        y_hg = _hgrn2(u, sb, lower[:, l], row(hg_head_norm[l]), lay=lay, cpt=plan["scan_cpt"], hgw=hgw)
        y_pool = _pool(u, w_pool[l].astype(BF16), row(pool_scale[l]), lay=lay, tm=plan["pool_tm"], hgw=hgw)
        h = _out_proj(h, y_hg, y_pool, w_out[l].astype(BF16), bm=plan["out_bm"])
        mlp_w = (row(norm_mlp[l]), w_up[l].astype(BF16), w_down[l].astype(BF16))
        if l < depth - 1:
            h = _mlp(h, *mlp_w, bm=plan["mlp_bm"], bf=mlp_bf)

    def final(row0, n_seq, s):
        bm = CHUNK * _largest_divisor(s // CHUNK, 8)
        return _mlp_final(h, *mlp_w, row(final_norm), row0=row0, n_seq=n_seq, seq_rows=s + lead,
                          s=s, bm=bm, bf=mlp_bf)

    return (final(0, lay.n_a, s_a), final(lay.n_a * lay.cpa * CHUNK, lay.n_b, s_b))
```

```python
import collections
import functools

import jax
import jax.numpy as jnp
import numpy as np
from jax import lax
from jax.experimental import pallas as pl
from jax.experimental.pallas import tpu as pltpu

N_META = 16
CHUNK = 64
META_PAD = (-N_META) % CHUNK
HEAD = 128
SUBLANES = 8
POOL_WINDOWS = (2, 4, 8, 16)
POOL_HALO = 8
EPS = 1e-6
FORGET_FLOOR = 1e-30
SAFE_LOG2_DECAY = 86.0
CLEAR_LOG2_DECAY = -1e30

VMEM_LIMIT_BYTES = 56 * 1024 * 1024

F32 = jnp.float32
BF16 = jnp.bfloat16

Layout = collections.namedtuple("Layout", "n_a cpa n_b cpb")


def _num_chunks(lay):
    return lay.n_a * lay.cpa + lay.n_b * lay.cpb


def _chunk_in_seq(cg, lay):
    na = lay.n_a * lay.cpa
    in_a = cg < na
    idx = jnp.where(in_a, lax.rem(cg, lay.cpa), lax.rem(jnp.maximum(cg - na, 0), lay.cpb))
    cps = jnp.where(in_a, lay.cpa, lay.cpb)
    return idx, cps


def _largest_divisor(n, cap):
    best = 1
    for d in range(1, n + 1):
        if n % d == 0 and d <= cap:
            best = d
    return best


def _rms(x, gain):
    ms = jnp.mean(x * x, axis=-1, keepdims=True)
    return x * lax.rsqrt(ms + EPS) * gain


def _sigmoid(x):
    return 1.0 / (1.0 + jnp.exp(-x))


def _silu(x):
    hx = 0.5 * x
    return hx * jnp.tanh(hx) + hx


def _dot(a, b):
    return jnp.dot(a, b, preferred_element_type=F32)


def _dot_nt(a, b):
    return lax.dot_general(a, b, (((1,), (1,)), ((), ())), preferred_element_type=F32)


def _dot_tn(a, b):
    return lax.dot_general(a, b, (((0,), (0,)), ((), ())), preferred_element_type=F32)


def _params(sem):
    return pltpu.CompilerParams(dimension_semantics=sem, vmem_limit_bytes=VMEM_LIMIT_BYTES)


def _norm_proj_kernel(h_ref, g_ref, w_ref, o_ref, a_scr):
    @pl.when(pl.program_id(1) == 0)
    def _():
        a_scr[...] = _rms(h_ref[...], g_ref[...]).astype(BF16)

    o_ref[...] = _dot(a_scr[...], w_ref[...])


def _norm_proj(h, gain, w, layer, *, bm, bn):
    rows, d = h.shape
    n = w.shape[2]
    return pl.pallas_call(
        _norm_proj_kernel,
        grid=(rows // bm, n // bn),
        in_specs=[
            pl.BlockSpec((bm, d), lambda i, j: (i, 0)),
            pl.BlockSpec((1, d), lambda i, j: (0, 0)),
            pl.BlockSpec((None, d, bn), lambda i, j: (layer, 0, j)),
        ],
        out_specs=pl.BlockSpec((bm, bn), lambda i, j: (i, j)),
        out_shape=jax.ShapeDtypeStruct((rows, n), F32),
        scratch_shapes=[pltpu.VMEM((bm, d), BF16)],
        compiler_params=_params(("parallel", "arbitrary")),
        name="norm_proj",
    )(h, gain, w)


def _forget(f_pre, lb):
    span = 1.0 - lb
    w = span * _sigmoid(f_pre)
    log2_f = jnp.log2(jnp.maximum(lb + w, FORGET_FLOOR))
    return log2_f, span - w


def _tri(lower):
    r = lax.broadcasted_iota(jnp.int32, (CHUNK, CHUNK), 0)
    c = lax.broadcasted_iota(jnp.int32, (CHUNK, CHUNK), 1)
    return (r >= c) if lower else (r <= c)


def _cumsum_rows(tri_bf16, g):
    hi = g.astype(BF16)
    lo = (g - hi.astype(F32)).astype(BF16)
    return _dot(tri_bf16, hi) + _dot(tri_bf16, lo)


def _pad_row_mask(is_first):
    r = lax.broadcasted_iota(jnp.int32, (CHUNK, 1), 0)
    return jnp.logical_or(jnp.logical_not(is_first), r >= META_PAD)


def _bwd_state_kernel(fb_ref, ip_ref, lb_ref, sb_ref, st_scr, *, lay, cpt, heads):
    tile = pl.num_programs(0) - 1 - pl.program_id(0)
    upper = jnp.where(_tri(False), 1.0, 0.0).astype(BF16)
    lb = lb_ref[...]

    @pl.when(pl.program_id(0) == 0)
    def _():
        st_scr[...] = jnp.zeros_like(st_scr)

    group = 2 if heads % 2 == 0 else 1
    slabs = [(c, g) for c in reversed(range(cpt)) for g in range(heads // group)]

    def stage_gates(c, g):
        idx, _ = _chunk_in_seq(tile * cpt + c, lay)
        rows = slice(c * CHUNK, (c + 1) * CHUNK)
        cols = slice(g * group * HEAD, (g + 1) * group * HEAD)
        log2_f, k = _forget(fb_ref[rows, cols], lb[:, cols])
        return dict(c=c, g=g, is_first=idx == 0, k=k,
                    v16=ip_ref[rows, cols].astype(BF16),
                    cb=_cumsum_rows(upper, log2_f))

    def stage_operands(x):
        c_end = jnp.where(x["is_first"], CLEAR_LOG2_DECAY, x["cb"][0:1, :])
        x["khat"] = (x["k"] * jnp.exp2(c_end - x["cb"])).astype(BF16)
        x["decay"] = jnp.exp2(c_end)

    def stage_update(x):
        for j in range(group):
            h = x["g"] * group + j
            hs = slice(h * HEAD, (h + 1) * HEAD)
            hl = slice(j * HEAD, (j + 1) * HEAD)
            st_old = st_scr[:, hs]
            sb_ref[x["c"], :, hs] = st_old.astype(BF16)
            st_scr[:, hs] = st_old * x["decay"][:, hl] + _dot_tn(x["v16"][:, hl], x["khat"][:, hl])

    stages = (stage_operands, stage_update)
    in_flight = []
    for step in range(len(slabs) + len(stages)):
        in_flight.insert(0, stage_gates(*slabs[step]) if step < len(slabs) else None)
        in_flight = in_flight[:len(stages) + 1]
        for depth, stage in enumerate(stages, start=1):
            if depth < len(in_flight) and in_flight[depth] is not None:
                stage(in_flight[depth])


def _bwd_states(u, lb_b, *, lay, cpt, hgw):
    nc = _num_chunks(lay)
    nt = nc // cpt
    tr = cpt * CHUNK
    heads = hgw // HEAD
    kern = functools.partial(_bwd_state_kernel, lay=lay, cpt=cpt, heads=heads)
    return pl.pallas_call(
        kern,
        grid=(nt,),
        in_specs=[
            pl.BlockSpec((tr, hgw), lambda j: (nt - 1 - j, 2)),
            pl.BlockSpec((tr, hgw), lambda j: (nt - 1 - j, 3)),
            pl.BlockSpec((1, hgw), lambda j: (0, 0)),
        ],
        out_specs=pl.BlockSpec((cpt, HEAD, hgw), lambda j: (nt - 1 - j, 0, 0)),
        out_shape=jax.ShapeDtypeStruct((nc, HEAD, hgw), BF16),
        scratch_shapes=[pltpu.VMEM((HEAD, hgw), F32)],
        compiler_params=_params(("arbitrary",)),
        name="hgrn2_bwd_states",
    )(u, u, lb_b)


def _chunk_gates(q_ref, ff_ref, fb_ref, ip_ref, rows, cols, lb_f, lb_b, lower, upper):
    q = _silu(q_ref[rows, cols])
    v16 = ip_ref[rows, cols].astype(BF16)
    gf, kf = _forget(ff_ref[rows, cols], lb_f[:, cols])
    gb, kb = _forget(fb_ref[rows, cols], lb_b[:, cols])
    bf = _cumsum_rows(lower, gf)
    cb = _cumsum_rows(upper, gb)
    return q, v16, kf, bf, kb, cb


def _head_output(o, gate, head_norm, valid, dtype):
    ms = jnp.mean(o * o, axis=-1, keepdims=True)
    y = o * lax.rsqrt(ms + EPS) * head_norm * gate
    return jnp.where(valid, y, 0.0).astype(dtype)


def _exact_scores(q_scr, k_scr, b_scr, hs, mask):
    qh = q_scr[:, hs]
    bh = b_scr[:, hs]
    lane = lax.broadcasted_iota(jnp.int32, (CHUNK, CHUNK), 1)

    def body(sg, acc):
        group = pl.ds(pl.multiple_of(sg * SUBLANES, SUBLANES), SUBLANES)
        k8 = k_scr[group, hs]
        b8 = b_scr[group, hs]
        for j in range(SUBLANES):
            e = jnp.exp2(jnp.minimum(bh - b8[j:j + 1, :], 0.0))
            col = jnp.sum(qh * k8[j:j + 1, :] * e, axis=1, keepdims=True)
            acc = jnp.where(lane == sg * SUBLANES + j, col, acc)
        return acc

    acc = lax.fori_loop(0, CHUNK // SUBLANES, body, jnp.zeros((CHUNK, CHUNK), F32))
    return jnp.where(mask, acc, 0.0)


def _hgrn2_kernel(q_ref, ff_ref, fb_ref, ip_ref, gt_ref, sb_ref, lb_ref, hn_ref, o_ref,
                  st_scr, oi_scr, redo_ref, q_scr, kf_scr, bf_scr, kb_scr, cb_scr,
                  *, lay, cpt, heads):
    tile = pl.program_id(0)
    lower_m = _tri(True)
    upper_m = _tri(False)
    lower = jnp.where(lower_m, 1.0, 0.0).astype(BF16)
    upper = jnp.where(upper_m, 1.0, 0.0).astype(BF16)
    lb_f = lb_ref[0:1, :]
    lb_b = lb_ref[1:2, :]
    hn = hn_ref[...]
    half = CHUNK // 2

    @pl.when(tile == 0)
    def _():
        st_scr[...] = jnp.zeros_like(st_scr)

    group = 2 if heads % 2 == 0 else 1
    slabs = [(c, g) for c in range(cpt) for g in range(heads // group)]
    margin = {}

    def stage_gates(c, g):
        idx, cps = _chunk_in_seq(tile * cpt + c, lay)
        rows = slice(c * CHUNK, (c + 1) * CHUNK)
        cols = slice(g * group * HEAD, (g + 1) * group * HEAD)
        q, v16, kf, bf, kb, cb = _chunk_gates(q_ref, ff_ref, fb_ref, ip_ref, rows, cols,
                                              lb_f, lb_b, lower, upper)
        return dict(c=c, g=g, rows=rows, valid=_pad_row_mask(idx == 0), is_last=idx == cps - 1,
                    q=q, v16=v16, kf=kf, bf=bf, kb=kb, cb=cb, gate=_silu(gt_ref[rows, cols]))

    def stage_operands(x):
        q, kf, bf, kb, cb = x["q"], x["kf"], x["bf"], x["kb"], x["cb"]
        b_last = bf[CHUNK - 1:CHUNK, :]
        c_first = cb[0:1, :]
        rf = bf[half - 1:half, :]
        rb = cb[half:half + 1, :]
        ef = jnp.exp2(bf - rf)
        eb = jnp.exp2(cb - rb)
        b_end = jnp.where(x["is_last"], CLEAR_LOG2_DECAY, b_last)
        m = jnp.minimum(jnp.minimum(rf, b_last - rf), jnp.minimum(rb, c_first - rb))
        margin[x["c"]] = m if x["g"] == 0 else jnp.minimum(margin[x["c"]], m)
        if x["g"] == heads // group - 1:
            redo_ref[x["c"]] = jnp.where(jnp.min(margin[x["c"]]) < -SAFE_LOG2_DECAY, 1, 0)
        x.update(
            qtf=(q * ef).astype(BF16),
            ktf=(kf * (1.0 / ef)).astype(BF16),
            qtb=(q * eb).astype(BF16),
            ktb=(kb * (1.0 / eb)).astype(BF16),
            qhf=(q * jnp.exp2(bf)).astype(BF16),
            qhb=(q * jnp.exp2(cb)).astype(BF16),
            khat=(kf * jnp.exp2(b_end - bf)).astype(BF16),
            decay=jnp.exp2(b_end),
        )

    def group_heads(x):
        for j in range(group):
            h = x["g"] * group + j
            yield j, slice(h * HEAD, (h + 1) * HEAD), slice(j * HEAD, (j + 1) * HEAD)

    def stage_scores(x):
        x["af"], x["ab"], x["oi"] = {}, {}, {}
        for j, hs, hl in group_heads(x):
            st_old = st_scr[:, hs]
            x["af"][j] = _dot_nt(x["qtf"][:, hl], x["ktf"][:, hl])
            x["ab"][j] = _dot_nt(x["qtb"][:, hl], x["ktb"][:, hl])
            qcat = jnp.concatenate([x["qhf"][:, hl], x["qhb"][:, hl]], axis=1)
            scat = jnp.concatenate([st_old.astype(BF16), sb_ref[x["c"], :, hs]], axis=1)
            x["oi"][j] = _dot_nt(qcat, scat)
            st_scr[:, hs] = st_old * x["decay"][:, hl] + _dot_tn(x["v16"][:, hl], x["khat"][:, hl])

    def stage_mix(x):
        x["o"] = {}
        for j, hs, hl in group_heads(x):
            a = (jnp.where(lower_m, x["af"][j], 0.0) + jnp.where(upper_m, x["ab"][j], 0.0)).astype(BF16)
            oi_scr[x["rows"], hs] = x["oi"][j]
            x["o"][j] = _dot(a, x["v16"][:, hl]) + x["oi"][j]

    def stage_output(x):
        for j, hs, hl in group_heads(x):
            o_ref[x["rows"], hs] = _head_output(x["o"][j], x["gate"][:, hl], hn[:, hs], x["valid"],
                                                o_ref.dtype)

    def chunk_redo(c):
        idx, _ = _chunk_in_seq(tile * cpt + c, lay)
        rows = pl.ds(pl.multiple_of(c * CHUNK, CHUNK), CHUNK)
        valid = _pad_row_mask(idx == 0)
        q, v16, kf, bf, kb, cb = _chunk_gates(q_ref, ff_ref, fb_ref, ip_ref, rows, slice(None),
                                              lb_f, lb_b, lower, upper)
        q_scr[...] = q
        kf_scr[...] = kf
        bf_scr[...] = bf
        kb_scr[...] = kb
        cb_scr[...] = cb
        gate = _silu(gt_ref[rows, :])
        for h in range(heads):
            hs = slice(h * HEAD, (h + 1) * HEAD)
            a = (_exact_scores(q_scr, kf_scr, bf_scr, hs, lower_m)
                 + _exact_scores(q_scr, kb_scr, cb_scr, hs, upper_m)).astype(BF16)
            o = _dot(a, v16[:, hs]) + oi_scr[rows, hs]
            o_ref[rows, hs] = _head_output(o, gate[:, hs], hn[:, hs], valid, o_ref.dtype)

    stages = (stage_operands, stage_scores, stage_mix, stage_output)
    in_flight = []
    for step in range(len(slabs) + len(stages)):
        in_flight.insert(0, stage_gates(*slabs[step]) if step < len(slabs) else None)
        in_flight = in_flight[:len(stages) + 1]
        for depth, stage in enumerate(stages, start=1):
            if depth < len(in_flight) and in_flight[depth] is not None:
                stage(in_flight[depth])

    def redo_body(c, carry):
        @pl.when(redo_ref[c] != 0)
        def _():
            chunk_redo(c)
        return carry

    lax.fori_loop(0, cpt, redo_body, 0)


def _hgrn2(u, sb, lb, head_norm, *, lay, cpt, hgw):
    nc = _num_chunks(lay)
    nt = nc // cpt
    tr = cpt * CHUNK
    heads = hgw // HEAD
    kern = functools.partial(_hgrn2_kernel, lay=lay, cpt=cpt, heads=heads)
    col = lambda part: pl.BlockSpec((tr, hgw), lambda i: (i, part))
    chunk_f32 = pltpu.VMEM((CHUNK, hgw), F32)
    return pl.pallas_call(
        kern,
        grid=(nt,),
        in_specs=[
            col(0), col(1), col(2), col(3), col(4),
            pl.BlockSpec((cpt, HEAD, hgw), lambda i: (i, 0, 0)),
            pl.BlockSpec((2, hgw), lambda i: (0, 0)),
            pl.BlockSpec((1, hgw), lambda i: (0, 0)),
        ],
        out_specs=pl.BlockSpec((tr, hgw), lambda i: (i, 0)),
        out_shape=jax.ShapeDtypeStruct((nc * CHUNK, hgw), BF16),
        scratch_shapes=[
            pltpu.VMEM((HEAD, hgw), F32),
            pltpu.VMEM((tr, hgw), F32),
            pltpu.SMEM((cpt,), jnp.int32),
            chunk_f32, chunk_f32, chunk_f32, chunk_f32, chunk_f32,
        ],
        compiler_params=_params(("arbitrary",)),
        name="hgrn2_scan",
    )(u, u, u, u, u, sb, lb, head_norm)


def _pool_bands(tm):
    r = np.arange(tm)[:, None]
    j = np.arange(tm + 2 * POOL_HALO)[None, :] - POOL_HALO
    bands = [(j >= r - w // 2) & (j < r + w - w // 2) for w in POOL_WINDOWS]
    return jnp.asarray(np.stack(bands), BF16)


def _pool_kernel(prev_ref, x_ref, next_ref, band_ref, wp_ref, ps_ref, o_ref, *, lay, tm):
    idx, cps = _chunk_in_seq(pl.program_id(0) * (tm // CHUNK), lay)
    p0 = idx * CHUNK
    seq_rows = cps * CHUNK
    pos_h = p0 - POOL_HALO + lax.broadcasted_iota(jnp.int32, (tm + 2 * POOL_HALO, 1), 0)
    valid_h = jnp.logical_and(pos_h >= META_PAD, pos_h < seq_rows)
    x_all = jnp.concatenate([prev_ref[...], x_ref[...], next_ref[...]], axis=0)
    xm = jnp.where(valid_h, x_all, 0.0)
    hi = xm.astype(BF16)
    lo = (xm - hi.astype(F32)).astype(BF16)

    pos = p0 + lax.broadcasted_iota(jnp.int32, (tm, 1), 0)
    valid = pos >= META_PAD
    gw = x_ref.shape[1] // len(POOL_WINDOWS)
    groups = range(len(POOL_WINDOWS))
    cols = [slice(g * gw, (g + 1) * gw) for g in groups]
    totals = [_dot(band_ref[g], hi[:, cols[g]]) + _dot(band_ref[g], lo[:, cols[g]]) for g in groups]
    pooled = []
    for g, window in enumerate(POOL_WINDOWS):
        back = window // 2
        count = (jnp.minimum(pos + (window - back), seq_rows) - jnp.maximum(pos - back, META_PAD))
        count = jnp.maximum(count, 1).astype(F32)
        pooled.append((totals[g] / count - xm[POOL_HALO:POOL_HALO + tm, cols[g]]).astype(BF16))
    ys = [_dot(pooled[g], wp_ref[g]) for g in groups]
    for g in groups:
        o_ref[:, cols[g]] = jnp.where(valid, ys[g] * ps_ref[:, cols[g]], 0.0).astype(o_ref.dtype)


def _pool(u, w_pool, layer, pool_scale, *, lay, tm, hgw):
    rows = u.shape[0]
    pw = pool_scale.shape[1]
    col = 5 * hgw // pw
    hb = tm // POOL_HALO
    last_hb = rows // POOL_HALO - 1
    bands = _pool_bands(tm)
    kern = functools.partial(_pool_kernel, lay=lay, tm=tm)
    return pl.pallas_call(
        kern,
        grid=(rows // tm,),
        in_specs=[
            pl.BlockSpec((POOL_HALO, pw), lambda i: (jnp.maximum(i * hb - 1, 0), col)),
            pl.BlockSpec((tm, pw), lambda i: (i, col)),
            pl.BlockSpec((POOL_HALO, pw), lambda i: (jnp.minimum((i + 1) * hb, last_hb), col)),
            pl.BlockSpec(bands.shape, lambda i: (0, 0, 0)),
            pl.BlockSpec((None,) + w_pool.shape[1:], lambda i: (layer, 0, 0, 0)),
            pl.BlockSpec((1, pw), lambda i: (0, 0)),
        ],
        out_specs=pl.BlockSpec((tm, pw), lambda i: (i, 0)),
        out_shape=jax.ShapeDtypeStruct((rows, pw), BF16),
        compiler_params=_params(("parallel",)),
        name="pool_mixer",
    )(u, u, u, bands, w_pool, pool_scale)


def _out_proj_kernel(h_ref, yh_ref, yp_ref, w_ref, o_ref):
    hgw = yh_ref.shape[1]
    o_ref[...] = (h_ref[...] + _dot(yh_ref[...], w_ref[0:hgw, :])
                  + _dot(yp_ref[...], w_ref[hgw:, :]))


def _out_proj(h, y_hg, y_pool, w_out, layer, *, bm):
    rows, d = h.shape
    hgw, pw = y_hg.shape[1], y_pool.shape[1]
    return pl.pallas_call(
        _out_proj_kernel,
        grid=(rows // bm,),
        in_specs=[
            pl.BlockSpec((bm, d), lambda i: (i, 0)),
            pl.BlockSpec((bm, hgw), lambda i: (i, 0)),
            pl.BlockSpec((bm, pw), lambda i: (i, 0)),
            pl.BlockSpec((None,) + w_out.shape[1:], lambda i: (layer, 0, 0)),
        ],
        out_specs=pl.BlockSpec((bm, d), lambda i: (i, 0)),
        out_shape=jax.ShapeDtypeStruct((rows, d), F32),
        compiler_params=_params(("parallel",)),
        name="out_proj",
    )(h, y_hg, y_pool, w_out)


def _mlp_kernel(h_ref, g_ref, wu_ref, wd_ref, fg_ref, o_ref, m_scr, *, final, f_axis):
    f = pl.program_id(f_axis)

    @pl.when(f == 0)
    def _():
        h = h_ref[...]
        m_scr[...] = _rms(h, g_ref[...]).astype(BF16)
        o_ref[...] = h

    hidden = jnp.square(jnp.maximum(_dot(m_scr[...], wu_ref[...]), 0.0)).astype(BF16)
    o_ref[...] += _dot(hidden, wd_ref[...])

    if final:
        @pl.when(f == pl.num_programs(f_axis) - 1)
        def _():
            o_ref[...] = _rms(o_ref[...], fg_ref[...])


def _mlp(h, gain, w_up, w_down, layer, *, bm, bf):
    rows, d = h.shape
    dff = w_up.shape[2]
    return pl.pallas_call(
        functools.partial(_mlp_kernel, final=False, f_axis=1),
        grid=(rows // bm, dff // bf),
        in_specs=[
            pl.BlockSpec((bm, d), lambda i, f: (i, 0)),
            pl.BlockSpec((1, d), lambda i, f: (0, 0)),
            pl.BlockSpec((None, d, bf), lambda i, f: (layer, 0, f)),
            pl.BlockSpec((None, bf, d), lambda i, f: (layer, f, 0)),
            pl.BlockSpec((1, d), lambda i, f: (0, 0)),
        ],
        out_specs=pl.BlockSpec((bm, d), lambda i, f: (i, 0)),
        out_shape=jax.ShapeDtypeStruct((rows, d), F32),
        scratch_shapes=[pltpu.VMEM((bm, d), BF16)],
        compiler_params=_params(("parallel", "arbitrary")),
        name="mlp",
    )(h, gain, w_up, w_down, gain)


def _mlp_final(h, gain, w_up, w_down, layer, final_gain, *, row0, n_seq, seq_rows, s, bm, bf):
    d = h.shape[1]
    dff = w_up.shape[2]
    skip = seq_rows - s
    return pl.pallas_call(
        functools.partial(_mlp_kernel, final=True, f_axis=2),
        grid=(n_seq, s // bm, dff // bf),
        in_specs=[
            pl.BlockSpec((pl.Element(bm), pl.Element(d)),
                         lambda b, i, f: (pl.multiple_of(row0 + b * seq_rows + skip + i * bm, CHUNK), 0)),
            pl.BlockSpec((1, d), lambda b, i, f: (0, 0)),
            pl.BlockSpec((None, d, bf), lambda b, i, f: (layer, 0, f)),
            pl.BlockSpec((None, bf, d), lambda b, i, f: (layer, f, 0)),
            pl.BlockSpec((1, d), lambda b, i, f: (0, 0)),
        ],
        out_specs=pl.BlockSpec((None, bm, d), lambda b, i, f: (b, i, 0)),
        out_shape=jax.ShapeDtypeStruct((n_seq, s, d), F32),
        scratch_shapes=[pltpu.VMEM((bm, d), BF16)],
        compiler_params=_params(("parallel", "parallel", "arbitrary")),
        name="mlp_final",
    )(h, gain, w_up, w_down, final_gain)


def _pack_rows(xs, meta):
    d = meta.shape[1]
    lead = jnp.concatenate([jnp.zeros((META_PAD, d), F32), meta.astype(F32)], axis=0)
    pieces = []
    for x in xs:
        for b in range(x.shape[0]):
            pieces += [lead, x[b]]
    return jnp.concatenate(pieces, axis=0)


def _tile_plan(lay):
    nc = _num_chunks(lay)
    g = 1
    for d in range(1, min(lay.cpa, lay.cpb) + 1):
        if lay.cpa % d == 0 and lay.cpb % d == 0:
            g = d
    return dict(
        proj_bm=CHUNK * _largest_divisor(nc, 12),
        scan_cpt=_largest_divisor(nc, 12),
        pool_tm=CHUNK * _largest_divisor(g, 4),
        out_bm=CHUNK * _largest_divisor(nc, 10),
        mlp_bm=CHUNK * _largest_divisor(nc, 10),
    )


def kernel(x_prompt, x_sample, meta_tokens, w_in, w_pool, pool_scale, hg_lower_bound, hg_head_norm,
           w_out, norm_mix, norm_mlp, w_up, w_down, final_norm):
    depth, d, in_cols = w_in.shape
    pw = pool_scale.shape[1]
    hgw = hg_head_norm.shape[1]
    dff = w_up.shape[2]
    lead = N_META + META_PAD
    s_a, s_b = x_prompt.shape[1], x_sample.shape[1]
    assert in_cols == 5 * hgw + pw and hgw % HEAD == 0 and pw == hgw
    assert s_a % CHUNK == 0 and s_b % CHUNK == 0
    lay = Layout(x_prompt.shape[0], (s_a + lead) // CHUNK, x_sample.shape[0], (s_b + lead) // CHUNK)
    plan = _tile_plan(lay)
    proj_bn = in_cols // _largest_divisor(in_cols // 256, 4)
    mlp_bf = dff // _largest_divisor(dff // 256, 8)

    probs = jax.nn.softmax(hg_lower_bound.astype(F32), axis=1)
    lower = jnp.cumsum(probs, axis=1) - probs[:, :1]

    h = _pack_rows((x_prompt, x_sample), meta_tokens)
    row = lambda a: a.astype(F32).reshape(1, -1)
    w_in, w_pool, w_out, w_up, w_down = (w.astype(BF16) for w in (w_in, w_pool, w_out, w_up, w_down))
    for l in range(depth):
        u = _norm_proj(h, row(norm_mix[l]), w_in, l, bm=plan["proj_bm"], bn=proj_bn)
        sb = _bwd_states(u, lower[1:2, l], lay=lay, cpt=plan["scan_cpt"], hgw=hgw)
        y_hg = _hgrn2(u, sb, lower[:, l], row(hg_head_norm[l]), lay=lay, cpt=plan["scan_cpt"], hgw=hgw)
        y_pool = _pool(u, w_pool, l, row(pool_scale[l]), lay=lay, tm=plan["pool_tm"], hgw=hgw)
        h = _out_proj(h, y_hg, y_pool, w_out, l, bm=plan["out_bm"])
        mlp_w = (row(norm_mlp[l]), w_up, w_down, l)
        if l < depth - 1:
            h = _mlp(h, *mlp_w, bm=plan["mlp_bm"], bf=mlp_bf)

    def final(row0, n_seq, s):
        bm = CHUNK * _largest_divisor(s // CHUNK, 8)
        return _mlp_final(h, *mlp_w, row(final_norm), row0=row0, n_seq=n_seq, seq_rows=s + lead,
                          s=s, bm=bm, bf=mlp_bf)

    return (final(0, lay.n_a, s_a), final(lay.n_a * lay.cpa * CHUNK, lay.n_b, s_b))
```

```python
import collections
import functools

import jax
import jax.numpy as jnp
import numpy as np
from jax import lax
from jax.experimental import pallas as pl
from jax.experimental.pallas import tpu as pltpu

N_META = 16
CHUNK = 64
META_PAD = (-N_META) % CHUNK
HEAD = 128
SUBLANES = 8
POOL_WINDOWS = (2, 4, 8, 16)
POOL_HALO = 8
EPS = 1e-6
FORGET_FLOOR = 1e-30
SAFE_LOG2_DECAY = 86.0
CLEAR_LOG2_DECAY = -1e30

VMEM_LIMIT_BYTES = 56 * 1024 * 1024

F32 = jnp.float32
BF16 = jnp.bfloat16

Layout = collections.namedtuple("Layout", "n_a cpa n_b cpb")


def _num_chunks(lay):
    return lay.n_a * lay.cpa + lay.n_b * lay.cpb


def _chunk_in_seq(cg, lay):
    na = lay.n_a * lay.cpa
    in_a = cg < na
    idx = jnp.where(in_a, lax.rem(cg, lay.cpa), lax.rem(jnp.maximum(cg - na, 0), lay.cpb))
    cps = jnp.where(in_a, lay.cpa, lay.cpb)
    return idx, cps


def _largest_divisor(n, cap):
    best = 1
    for d in range(1, n + 1):
        if n % d == 0 and d <= cap:
            best = d
    return best


def _rms(x, gain):
    ms = jnp.mean(x * x, axis=-1, keepdims=True)
    return x * lax.rsqrt(ms + EPS) * gain


def _sigmoid(x):
    return 1.0 / (1.0 + jnp.exp(-x))


def _silu(x):
    hx = 0.5 * x
    return hx * jnp.tanh(hx) + hx


def _dot(a, b):
    return jnp.dot(a, b, preferred_element_type=F32)


def _dot_nt(a, b):
    return lax.dot_general(a, b, (((1,), (1,)), ((), ())), preferred_element_type=F32)


def _dot_tn(a, b):
    return lax.dot_general(a, b, (((0,), (0,)), ((), ())), preferred_element_type=F32)


def _params(sem):
    return pltpu.CompilerParams(dimension_semantics=sem, vmem_limit_bytes=VMEM_LIMIT_BYTES)


def _norm_proj_kernel(h_ref, g_ref, w_ref, o_ref, a_scr):
    @pl.when(pl.program_id(1) == 0)
    def _():
        a = _rms(h_ref[...], g_ref[...]).astype(BF16)
        a_scr[...] = a
        o_ref[...] = _dot(a, w_ref[...])

    @pl.when(pl.program_id(1) > 0)
    def _():
        o_ref[...] = _dot(a_scr[...], w_ref[...])


def _norm_proj(h, gain, w, layer, *, bm, bn):
    rows, d = h.shape
    n = w.shape[2]
    return pl.pallas_call(
        _norm_proj_kernel,
        grid=(rows // bm, n // bn),
        in_specs=[
            pl.BlockSpec((bm, d), lambda i, j: (i, 0)),
            pl.BlockSpec((1, d), lambda i, j: (0, 0)),
            pl.BlockSpec((None, d, bn), lambda i, j: (layer, 0, j)),
        ],
        out_specs=pl.BlockSpec((bm, bn), lambda i, j: (i, j)),
        out_shape=jax.ShapeDtypeStruct((rows, n), F32),
        scratch_shapes=[pltpu.VMEM((bm, d), BF16)],
        compiler_params=_params(("parallel", "arbitrary")),
        name="norm_proj",
    )(h, gain, w)


def _forget(f_pre, lb):
    span = 1.0 - lb
    w = span * _sigmoid(f_pre)
    log2_f = jnp.log2(jnp.maximum(lb + w, FORGET_FLOOR))
    return log2_f, span - w


def _tri(lower):
    r = lax.broadcasted_iota(jnp.int32, (CHUNK, CHUNK), 0)
    c = lax.broadcasted_iota(jnp.int32, (CHUNK, CHUNK), 1)
    return (r >= c) if lower else (r <= c)


def _cumsum_rows(tri_bf16, g):
    hi = g.astype(BF16)
    lo = (g - hi.astype(F32)).astype(BF16)
    return _dot(tri_bf16, hi) + _dot(tri_bf16, lo)


def _pad_row_mask(is_first):
    r = lax.broadcasted_iota(jnp.int32, (CHUNK, 1), 0)
    return jnp.logical_or(jnp.logical_not(is_first), r >= META_PAD)


def _bwd_state_kernel(fb_ref, ip_ref, lb_ref, sb_ref, st_scr, *, lay, cpt, heads):
    tile = pl.num_programs(0) - 1 - pl.program_id(0)
    upper = jnp.where(_tri(False), 1.0, 0.0).astype(BF16)
    lb = lb_ref[...]

    @pl.when(pl.program_id(0) == 0)
    def _():
        st_scr[...] = jnp.zeros_like(st_scr)

    group = 2 if heads % 2 == 0 else 1
    slabs = [(c, g) for c in reversed(range(cpt)) for g in range(heads // group)]

    def stage_gates(c, g):
        idx, _ = _chunk_in_seq(tile * cpt + c, lay)
        rows = slice(c * CHUNK, (c + 1) * CHUNK)
        cols = slice(g * group * HEAD, (g + 1) * group * HEAD)
        log2_f, k = _forget(fb_ref[rows, cols], lb[:, cols])
        return dict(c=c, g=g, is_first=idx == 0, k=k,
                    v16=ip_ref[rows, cols].astype(BF16),
                    cb=_cumsum_rows(upper, log2_f))

    def stage_operands(x):
        c_end = jnp.where(x["is_first"], CLEAR_LOG2_DECAY, x["cb"][0:1, :])
        x["khat"] = (x["k"] * jnp.exp2(c_end - x["cb"])).astype(BF16)
        x["decay"] = jnp.exp2(c_end)

    def stage_update(x):
        for j in range(group):
            h = x["g"] * group + j
            hs = slice(h * HEAD, (h + 1) * HEAD)
            hl = slice(j * HEAD, (j + 1) * HEAD)
            st_old = st_scr[:, hs]
            sb_ref[x["c"], :, hs] = st_old.astype(BF16)
            st_scr[:, hs] = st_old * x["decay"][:, hl] + _dot_tn(x["v16"][:, hl], x["khat"][:, hl])

    stages = (stage_operands, stage_update)
    in_flight = []
    for step in range(len(slabs) + len(stages)):
        in_flight.insert(0, stage_gates(*slabs[step]) if step < len(slabs) else None)
        in_flight = in_flight[:len(stages) + 1]
        for depth, stage in enumerate(stages, start=1):
            if depth < len(in_flight) and in_flight[depth] is not None:
                stage(in_flight[depth])


def _bwd_states(u, lb_b, *, lay, cpt, hgw):
    nc = _num_chunks(lay)
    nt = nc // cpt
    tr = cpt * CHUNK
    heads = hgw // HEAD
    kern = functools.partial(_bwd_state_kernel, lay=lay, cpt=cpt, heads=heads)
    return pl.pallas_call(
        kern,
        grid=(nt,),
        in_specs=[
            pl.BlockSpec((tr, hgw), lambda j: (nt - 1 - j, 2)),
            pl.BlockSpec((tr, hgw), lambda j: (nt - 1 - j, 3)),
            pl.BlockSpec((1, hgw), lambda j: (0, 0)),
        ],
        out_specs=pl.BlockSpec((cpt, HEAD, hgw), lambda j: (nt - 1 - j, 0, 0)),
        out_shape=jax.ShapeDtypeStruct((nc, HEAD, hgw), BF16),
        scratch_shapes=[pltpu.VMEM((HEAD, hgw), F32)],
        compiler_params=_params(("arbitrary",)),
        name="hgrn2_bwd_states",
    )(u, u, lb_b)


def _chunk_gates(q_ref, ff_ref, fb_ref, ip_ref, rows, cols, lb_f, lb_b, lower, upper):
    q = _silu(q_ref[rows, cols])
    v16 = ip_ref[rows, cols].astype(BF16)
    gf, kf = _forget(ff_ref[rows, cols], lb_f[:, cols])
    gb, kb = _forget(fb_ref[rows, cols], lb_b[:, cols])
    bf = _cumsum_rows(lower, gf)
    cb = _cumsum_rows(upper, gb)
    return q, v16, kf, bf, kb, cb


def _head_output(o, gate, head_norm, valid, dtype):
    ms = jnp.mean(o * o, axis=-1, keepdims=True)
    y = o * lax.rsqrt(ms + EPS) * head_norm * gate
    return jnp.where(valid, y, 0.0).astype(dtype)


def _exact_scores(q_scr, k_scr, b_scr, hs, mask):
    qh = q_scr[:, hs]
    bh = b_scr[:, hs]
    lane = lax.broadcasted_iota(jnp.int32, (CHUNK, CHUNK), 1)

    def body(sg, acc):
        group = pl.ds(pl.multiple_of(sg * SUBLANES, SUBLANES), SUBLANES)
        k8 = k_scr[group, hs]
        b8 = b_scr[group, hs]
        for j in range(SUBLANES):
            e = jnp.exp2(jnp.minimum(bh - b8[j:j + 1, :], 0.0))
            col = jnp.sum(qh * k8[j:j + 1, :] * e, axis=1, keepdims=True)
            acc = jnp.where(lane == sg * SUBLANES + j, col, acc)
        return acc

    acc = lax.fori_loop(0, CHUNK // SUBLANES, body, jnp.zeros((CHUNK, CHUNK), F32))
    return jnp.where(mask, acc, 0.0)


def _hgrn2_kernel(q_ref, ff_ref, fb_ref, ip_ref, gt_ref, sb_ref, lb_ref, hn_ref, o_ref,
                  st_scr, oi_scr, redo_ref, q_scr, kf_scr, bf_scr, kb_scr, cb_scr,
                  *, lay, cpt, heads):
    tile = pl.program_id(0)
    lower_m = _tri(True)
    upper_m = _tri(False)
    lower = jnp.where(lower_m, 1.0, 0.0).astype(BF16)
    upper = jnp.where(upper_m, 1.0, 0.0).astype(BF16)
    lb_f = lb_ref[0:1, :]
    lb_b = lb_ref[1:2, :]
    hn = hn_ref[...]
    half = CHUNK // 2

    @pl.when(tile == 0)
    def _():
        st_scr[...] = jnp.zeros_like(st_scr)

    group = 2 if heads % 2 == 0 else 1
    slabs = [(c, g) for c in range(cpt) for g in range(heads // group)]
    margin = {}

    def stage_gates(c, g):
        idx, cps = _chunk_in_seq(tile * cpt + c, lay)
        rows = slice(c * CHUNK, (c + 1) * CHUNK)
        cols = slice(g * group * HEAD, (g + 1) * group * HEAD)
        q, v16, kf, bf, kb, cb = _chunk_gates(q_ref, ff_ref, fb_ref, ip_ref, rows, cols,
                                              lb_f, lb_b, lower, upper)
        return dict(c=c, g=g, rows=rows, valid=_pad_row_mask(idx == 0), is_last=idx == cps - 1,
                    q=q, v16=v16, kf=kf, bf=bf, kb=kb, cb=cb, gate=_silu(gt_ref[rows, cols]))

    def stage_operands(x):
        q, kf, bf, kb, cb = x["q"], x["kf"], x["bf"], x["kb"], x["cb"]
        b_last = bf[CHUNK - 1:CHUNK, :]
        c_first = cb[0:1, :]
        rf = bf[half - 1:half, :]
        rb = cb[half:half + 1, :]
        ef = jnp.exp2(bf - rf)
        eb = jnp.exp2(cb - rb)
        b_end = jnp.where(x["is_last"], CLEAR_LOG2_DECAY, b_last)
        m = jnp.minimum(jnp.minimum(rf, b_last - rf), jnp.minimum(rb, c_first - rb))
        margin[x["c"]] = m if x["g"] == 0 else jnp.minimum(margin[x["c"]], m)
        if x["g"] == heads // group - 1:
            redo_ref[x["c"]] = jnp.where(jnp.min(margin[x["c"]]) < -SAFE_LOG2_DECAY, 1, 0)
        x.update(
            qtf=(q * ef).astype(BF16),
            ktf=(kf * (1.0 / ef)).astype(BF16),
            qtb=(q * eb).astype(BF16),
            ktb=(kb * (1.0 / eb)).astype(BF16),
            qhf=(q * jnp.exp2(bf)).astype(BF16),
            qhb=(q * jnp.exp2(cb)).astype(BF16),
            khat=(kf * jnp.exp2(b_end - bf)).astype(BF16),
            decay=jnp.exp2(b_end),
        )

    def group_heads(x):
        for j in range(group):
            h = x["g"] * group + j
            yield j, slice(h * HEAD, (h + 1) * HEAD), slice(j * HEAD, (j + 1) * HEAD)

    def stage_scores(x):
        x["af"], x["ab"], x["oi"] = {}, {}, {}
        for j, hs, hl in group_heads(x):
            st_old = st_scr[:, hs]
            x["af"][j] = _dot_nt(x["qtf"][:, hl], x["ktf"][:, hl])
            x["ab"][j] = _dot_nt(x["qtb"][:, hl], x["ktb"][:, hl])
            qcat = jnp.concatenate([x["qhf"][:, hl], x["qhb"][:, hl]], axis=1)
            scat = jnp.concatenate([st_old.astype(BF16), sb_ref[x["c"], :, hs]], axis=1)
            x["oi"][j] = _dot_nt(qcat, scat)
            st_scr[:, hs] = st_old * x["decay"][:, hl] + _dot_tn(x["v16"][:, hl], x["khat"][:, hl])

    def stage_mix(x):
        x["o"] = {}
        for j, hs, hl in group_heads(x):
            a = (jnp.where(lower_m, x["af"][j], 0.0) + jnp.where(upper_m, x["ab"][j], 0.0)).astype(BF16)
            oi_scr[x["rows"], hs] = x["oi"][j]
            x["o"][j] = _dot(a, x["v16"][:, hl]) + x["oi"][j]

    def stage_output(x):
        for j, hs, hl in group_heads(x):
            o_ref[x["rows"], hs] = _head_output(x["o"][j], x["gate"][:, hl], hn[:, hs], x["valid"],
                                                o_ref.dtype)

    def chunk_redo(c):
        idx, _ = _chunk_in_seq(tile * cpt + c, lay)
        rows = pl.ds(pl.multiple_of(c * CHUNK, CHUNK), CHUNK)
        valid = _pad_row_mask(idx == 0)
        q, v16, kf, bf, kb, cb = _chunk_gates(q_ref, ff_ref, fb_ref, ip_ref, rows, slice(None),
                                              lb_f, lb_b, lower, upper)
        q_scr[...] = q
        kf_scr[...] = kf
        bf_scr[...] = bf
        kb_scr[...] = kb
        cb_scr[...] = cb
        gate = _silu(gt_ref[rows, :])
        for h in range(heads):
            hs = slice(h * HEAD, (h + 1) * HEAD)
            a = (_exact_scores(q_scr, kf_scr, bf_scr, hs, lower_m)
                 + _exact_scores(q_scr, kb_scr, cb_scr, hs, upper_m)).astype(BF16)
            o = _dot(a, v16[:, hs]) + oi_scr[rows, hs]
            o_ref[rows, hs] = _head_output(o, gate[:, hs], hn[:, hs], valid, o_ref.dtype)

    stages = (stage_operands, stage_scores, stage_mix, stage_output)
    in_flight = []
    for step in range(len(slabs) + len(stages)):
        in_flight.insert(0, stage_gates(*slabs[step]) if step < len(slabs) else None)
        in_flight = in_flight[:len(stages) + 1]
        for depth, stage in enumerate(stages, start=1):
            if depth < len(in_flight) and in_flight[depth] is not None:
                stage(in_flight[depth])

    def redo_body(c, carry):
        @pl.when(redo_ref[c] != 0)
        def _():
            chunk_redo(c)
        return carry

    lax.fori_loop(0, cpt, redo_body, 0)


def _hgrn2(u, sb, lb, head_norm, *, lay, cpt, hgw):
    nc = _num_chunks(lay)
    nt = nc // cpt
    tr = cpt * CHUNK
    heads = hgw // HEAD
    kern = functools.partial(_hgrn2_kernel, lay=lay, cpt=cpt, heads=heads)
    col = lambda part: pl.BlockSpec((tr, hgw), lambda i: (i, part))
    chunk_f32 = pltpu.VMEM((CHUNK, hgw), F32)
    return pl.pallas_call(
        kern,
        grid=(nt,),
        in_specs=[
            col(0), col(1), col(2), col(3), col(4),
            pl.BlockSpec((cpt, HEAD, hgw), lambda i: (i, 0, 0)),
            pl.BlockSpec((2, hgw), lambda i: (0, 0)),
            pl.BlockSpec((1, hgw), lambda i: (0, 0)),
        ],
        out_specs=pl.BlockSpec((tr, hgw), lambda i: (i, 0)),
        out_shape=jax.ShapeDtypeStruct((nc * CHUNK, hgw), BF16),
        scratch_shapes=[
            pltpu.VMEM((HEAD, hgw), F32),
            pltpu.VMEM((tr, hgw), F32),
            pltpu.SMEM((cpt,), jnp.int32),
            chunk_f32, chunk_f32, chunk_f32, chunk_f32, chunk_f32,
        ],
        compiler_params=_params(("arbitrary",)),
        name="hgrn2_scan",
    )(u, u, u, u, u, sb, lb, head_norm)


def _pool_bands(tm):
    r = np.arange(tm)[:, None]
    j = np.arange(tm + 2 * POOL_HALO)[None, :] - POOL_HALO
    bands = [(j >= r - w // 2) & (j < r + w - w // 2) for w in POOL_WINDOWS]
    return jnp.asarray(np.stack(bands), BF16)


def _pool_kernel(prev_ref, x_ref, next_ref, band_ref, wp_ref, ps_ref, o_ref, *, lay, tm, sub):
    gw = x_ref.shape[1] // len(POOL_WINDOWS)
    groups = range(len(POOL_WINDOWS))
    cols = [slice(g * gw, (g + 1) * gw) for g in groups]

    def masked_rows(t):
        r0 = t * tm
        idx, cps = _chunk_in_seq((pl.program_id(0) * sub + t) * (tm // CHUNK), lay)
        p0 = idx * CHUNK
        seq_rows = cps * CHUNK
        pos_h = p0 - POOL_HALO + lax.broadcasted_iota(jnp.int32, (tm + 2 * POOL_HALO, 1), 0)
        valid_h = jnp.logical_and(pos_h >= META_PAD, pos_h < seq_rows)
        before = prev_ref[...] if t == 0 else x_ref[r0 - POOL_HALO:r0, :]
        after = next_ref[...] if t == sub - 1 else x_ref[r0 + tm:r0 + tm + POOL_HALO, :]
        x_all = jnp.concatenate([before, x_ref[r0:r0 + tm, :], after], axis=0)
        xm = jnp.where(valid_h, x_all, 0.0)
        hi = xm.astype(BF16)
        lo = (xm - hi.astype(F32)).astype(BF16)
        return dict(r0=r0, pos=p0 + lax.broadcasted_iota(jnp.int32, (tm, 1), 0), seq_rows=seq_rows,
                    xm=xm, hi=hi, lo=lo)

    tiles = [masked_rows(t) for t in range(sub)]
    for x in tiles:
        x["total"] = [_dot(band_ref[g], x["hi"][:, cols[g]]) + _dot(band_ref[g], x["lo"][:, cols[g]])
                      for g in groups]
    for x in tiles:
        x["pooled"] = []
        for g, window in enumerate(POOL_WINDOWS):
            back = window // 2
            count = (jnp.minimum(x["pos"] + (window - back), x["seq_rows"])
                     - jnp.maximum(x["pos"] - back, META_PAD))
            count = jnp.maximum(count, 1).astype(F32)
            centre = x["xm"][POOL_HALO:POOL_HALO + tm, cols[g]]
            x["pooled"].append((x["total"][g] / count - centre).astype(BF16))
    for x in tiles:
        x["y"] = [_dot(x["pooled"][g], wp_ref[g]) for g in groups]
    for x in tiles:
        valid = x["pos"] >= META_PAD
        for g in groups:
            y = jnp.where(valid, x["y"][g] * ps_ref[:, cols[g]], 0.0)
            o_ref[x["r0"]:x["r0"] + tm, cols[g]] = y.astype(o_ref.dtype)


def _pool(u, w_pool, layer, pool_scale, *, lay, tm, sub, hgw):
    rows = u.shape[0]
    pw = pool_scale.shape[1]
    col = 5 * hgw // pw
    bt = tm * sub
    hb = bt // POOL_HALO
    last_hb = rows // POOL_HALO - 1
    bands = _pool_bands(tm)
    kern = functools.partial(_pool_kernel, lay=lay, tm=tm, sub=sub)
    return pl.pallas_call(
        kern,
        grid=(rows // bt,),
        in_specs=[
            pl.BlockSpec((POOL_HALO, pw), lambda i: (jnp.maximum(i * hb - 1, 0), col)),
            pl.BlockSpec((bt, pw), lambda i: (i, col)),
            pl.BlockSpec((POOL_HALO, pw), lambda i: (jnp.minimum((i + 1) * hb, last_hb), col)),
            pl.BlockSpec(bands.shape, lambda i: (0, 0, 0)),
            pl.BlockSpec((None,) + w_pool.shape[1:], lambda i: (layer, 0, 0, 0)),
            pl.BlockSpec((1, pw), lambda i: (0, 0)),
        ],
        out_specs=pl.BlockSpec((bt, pw), lambda i: (i, 0)),
        out_shape=jax.ShapeDtypeStruct((rows, pw), BF16),
        compiler_params=_params(("parallel",)),
        name="pool_mixer",
    )(u, u, u, bands, w_pool, pool_scale)


def _out_proj_kernel(h_ref, yh_ref, yp_ref, w_ref, o_ref):
    hgw = yh_ref.shape[1]
    o_ref[...] = (h_ref[...] + _dot(yh_ref[...], w_ref[0:hgw, :])
                  + _dot(yp_ref[...], w_ref[hgw:, :]))


def _out_proj(h, y_hg, y_pool, w_out, layer, *, bm):
    rows, d = h.shape
    hgw, pw = y_hg.shape[1], y_pool.shape[1]
    return pl.pallas_call(
        _out_proj_kernel,
        grid=(rows // bm,),
        in_specs=[
            pl.BlockSpec((bm, d), lambda i: (i, 0)),
            pl.BlockSpec((bm, hgw), lambda i: (i, 0)),
            pl.BlockSpec((bm, pw), lambda i: (i, 0)),
            pl.BlockSpec((None,) + w_out.shape[1:], lambda i: (layer, 0, 0)),
        ],
        out_specs=pl.BlockSpec((bm, d), lambda i: (i, 0)),
        out_shape=jax.ShapeDtypeStruct((rows, d), F32),
        compiler_params=_params(("parallel",)),
        name="out_proj",
    )(h, y_hg, y_pool, w_out)


def _mlp_kernel(h_ref, g_ref, wu_ref, wd_ref, fg_ref, o_ref, m_scr, *, final, f_axis):
    f = pl.program_id(f_axis)

    def contribution(m):
        hidden = jnp.square(jnp.maximum(_dot(m, wu_ref[...]), 0.0)).astype(BF16)
        return _dot(hidden, wd_ref[...])

    @pl.when(f == 0)
    def _():
        h = h_ref[...]
        m = _rms(h, g_ref[...]).astype(BF16)
        m_scr[...] = m
        o_ref[...] = h + contribution(m)

    @pl.when(f > 0)
    def _():
        o_ref[...] += contribution(m_scr[...])

    if final:
        @pl.when(f == pl.num_programs(f_axis) - 1)
        def _():
            o_ref[...] = _rms(o_ref[...], fg_ref[...])


def _mlp(h, gain, w_up, w_down, layer, *, bm, bf):
    rows, d = h.shape
    dff = w_up.shape[2]
    return pl.pallas_call(
        functools.partial(_mlp_kernel, final=False, f_axis=1),
        grid=(rows // bm, dff // bf),
        in_specs=[
            pl.BlockSpec((bm, d), lambda i, f: (i, 0)),
            pl.BlockSpec((1, d), lambda i, f: (0, 0)),
            pl.BlockSpec((None, d, bf), lambda i, f: (layer, 0, f)),
            pl.BlockSpec((None, bf, d), lambda i, f: (layer, f, 0)),
            pl.BlockSpec((1, d), lambda i, f: (0, 0)),
        ],
        out_specs=pl.BlockSpec((bm, d), lambda i, f: (i, 0)),
        out_shape=jax.ShapeDtypeStruct((rows, d), F32),
        scratch_shapes=[pltpu.VMEM((bm, d), BF16)],
        compiler_params=_params(("parallel", "arbitrary")),
        name="mlp",
    )(h, gain, w_up, w_down, gain)


def _mlp_final(h, gain, w_up, w_down, layer, final_gain, *, row0, n_seq, seq_rows, s, bm, bf):
    d = h.shape[1]
    dff = w_up.shape[2]
    skip = seq_rows - s
    return pl.pallas_call(
        functools.partial(_mlp_kernel, final=True, f_axis=2),
        grid=(n_seq, s // bm, dff // bf),
        in_specs=[
            pl.BlockSpec((pl.Element(bm), pl.Element(d)),
                         lambda b, i, f: (pl.multiple_of(row0 + b * seq_rows + skip + i * bm, CHUNK), 0)),
            pl.BlockSpec((1, d), lambda b, i, f: (0, 0)),
            pl.BlockSpec((None, d, bf), lambda b, i, f: (layer, 0, f)),
            pl.BlockSpec((None, bf, d), lambda b, i, f: (layer, f, 0)),
            pl.BlockSpec((1, d), lambda b, i, f: (0, 0)),
        ],
        out_specs=pl.BlockSpec((None, bm, d), lambda b, i, f: (b, i, 0)),
        out_shape=jax.ShapeDtypeStruct((n_seq, s, d), F32),
        scratch_shapes=[pltpu.VMEM((bm, d), BF16)],
        compiler_params=_params(("parallel", "parallel", "arbitrary")),
        name="mlp_final",
    )(h, gain, w_up, w_down, final_gain)


def _pack_rows(xs, meta):
    d = meta.shape[1]
    lead = jnp.concatenate([jnp.zeros((META_PAD, d), F32), meta.astype(F32)], axis=0)
    pieces = []
    for x in xs:
        for b in range(x.shape[0]):
            pieces += [lead, x[b]]
    return jnp.concatenate(pieces, axis=0)


def _tile_plan(lay):
    nc = _num_chunks(lay)
    g = 1
    for d in range(1, min(lay.cpa, lay.cpb) + 1):
        if lay.cpa % d == 0 and lay.cpb % d == 0:
            g = d
    return dict(
        proj_bm=CHUNK * _largest_divisor(nc, 12),
        scan_cpt=_largest_divisor(nc, 12),
        pool_tm=CHUNK * _largest_divisor(g, 4),
        pool_sub=_largest_divisor(nc // _largest_divisor(g, 4), 4),
        out_bm=CHUNK * _largest_divisor(nc, 10),
        mlp_bm=CHUNK * _largest_divisor(nc, 10),
    )


def kernel(x_prompt, x_sample, meta_tokens, w_in, w_pool, pool_scale, hg_lower_bound, hg_head_norm,
           w_out, norm_mix, norm_mlp, w_up, w_down, final_norm):
    depth, d, in_cols = w_in.shape
    pw = pool_scale.shape[1]
    hgw = hg_head_norm.shape[1]
    dff = w_up.shape[2]
    lead = N_META + META_PAD
    s_a, s_b = x_prompt.shape[1], x_sample.shape[1]
    assert in_cols == 5 * hgw + pw and hgw % HEAD == 0 and pw == hgw
    assert s_a % CHUNK == 0 and s_b % CHUNK == 0
    lay = Layout(x_prompt.shape[0], (s_a + lead) // CHUNK, x_sample.shape[0], (s_b + lead) // CHUNK)
    plan = _tile_plan(lay)
    proj_bn = in_cols // _largest_divisor(in_cols // 256, 4)
    mlp_bf = dff // _largest_divisor(dff // 256, 8)

    probs = jax.nn.softmax(hg_lower_bound.astype(F32), axis=1)
    lower = jnp.cumsum(probs, axis=1) - probs[:, :1]

    h = _pack_rows((x_prompt, x_sample), meta_tokens)
    row = lambda a: a.astype(F32).reshape(1, -1)
    w_in, w_pool, w_out, w_up, w_down = (w.astype(BF16) for w in (w_in, w_pool, w_out, w_up, w_down))
    for l in range(depth):
        u = _norm_proj(h, row(norm_mix[l]), w_in, l, bm=plan["proj_bm"], bn=proj_bn)
        sb = _bwd_states(u, lower[1:2, l], lay=lay, cpt=plan["scan_cpt"], hgw=hgw)
        y_hg = _hgrn2(u, sb, lower[:, l], row(hg_head_norm[l]), lay=lay, cpt=plan["scan_cpt"], hgw=hgw)
        y_pool = _pool(u, w_pool, l, row(pool_scale[l]), lay=lay, tm=plan["pool_tm"],
                       sub=plan["pool_sub"], hgw=hgw)
        h = _out_proj(h, y_hg, y_pool, w_out, l, bm=plan["out_bm"])
        mlp_w = (row(norm_mlp[l]), w_up, w_down, l)
        if l < depth - 1:
            h = _mlp(h, *mlp_w, bm=plan["mlp_bm"], bf=mlp_bf)

    def final(row0, n_seq, s):
        bm = CHUNK * _largest_divisor(s // CHUNK, 8)
        return _mlp_final(h, *mlp_w, row(final_norm), row0=row0, n_seq=n_seq, seq_rows=s + lead,
                          s=s, bm=bm, bf=2 * mlp_bf)

    return (final(0, lay.n_a, s_a), final(lay.n_a * lay.cpa * CHUNK, lay.n_b, s_b))
```

```python
import collections
import functools

import jax
import jax.numpy as jnp
import numpy as np
from jax import lax
from jax.experimental import pallas as pl
from jax.experimental.pallas import tpu as pltpu

N_META = 16
CHUNK = 64
META_PAD = (-N_META) % CHUNK
HEAD = 128
SUBLANES = 8
POOL_WINDOWS = (2, 4, 8, 16)
POOL_HALO = 8
EPS = 1e-6
FORGET_FLOOR = 1e-30
SAFE_LOG2_DECAY = 86.0
CLEAR_LOG2_DECAY = -1e30

VMEM_LIMIT_BYTES = 56 * 1024 * 1024

F32 = jnp.float32
BF16 = jnp.bfloat16

Layout = collections.namedtuple("Layout", "n_a cpa n_b cpb")


def _num_chunks(lay):
    return lay.n_a * lay.cpa + lay.n_b * lay.cpb


def _chunk_in_seq(cg, lay):
    na = lay.n_a * lay.cpa
    in_a = cg < na
    idx = jnp.where(in_a, lax.rem(cg, lay.cpa), lax.rem(jnp.maximum(cg - na, 0), lay.cpb))
    cps = jnp.where(in_a, lay.cpa, lay.cpb)
    return idx, cps


def _largest_divisor(n, cap):
    best = 1
    for d in range(1, n + 1):
        if n % d == 0 and d <= cap:
            best = d
    return best


def _rms(x, gain):
    ms = jnp.mean(x * x, axis=-1, keepdims=True)
    return x * lax.rsqrt(ms + EPS) * gain


def _sigmoid(x):
    return 1.0 / (1.0 + jnp.exp(-x))


def _silu(x):
    hx = 0.5 * x
    return hx * jnp.tanh(hx) + hx


def _dot(a, b):
    return jnp.dot(a, b, preferred_element_type=F32)


def _dot_nt(a, b):
    return lax.dot_general(a, b, (((1,), (1,)), ((), ())), preferred_element_type=F32)


def _dot_tn(a, b):
    return lax.dot_general(a, b, (((0,), (0,)), ((), ())), preferred_element_type=F32)


def _params(sem):
    return pltpu.CompilerParams(dimension_semantics=sem, vmem_limit_bytes=VMEM_LIMIT_BYTES)


def _norm_proj_kernel(h_ref, g_ref, w_ref, o_ref, a_scr):
    @pl.when(pl.program_id(1) == 0)
    def _():
        a = _rms(h_ref[...], g_ref[...]).astype(BF16)
        a_scr[...] = a
        o_ref[...] = _dot(a, w_ref[...])

    @pl.when(pl.program_id(1) > 0)
    def _():
        o_ref[...] = _dot(a_scr[...], w_ref[...])


def _norm_proj(h, gain, w, layer, *, bm, bn):
    rows, d = h.shape
    n = w.shape[2]
    return pl.pallas_call(
        _norm_proj_kernel,
        grid=(rows // bm, n // bn),
        in_specs=[
            pl.BlockSpec((bm, d), lambda i, j: (i, 0)),
            pl.BlockSpec((1, d), lambda i, j: (0, 0)),
            pl.BlockSpec((None, d, bn), lambda i, j: (layer, 0, j)),
        ],
        out_specs=pl.BlockSpec((bm, bn), lambda i, j: (i, j)),
        out_shape=jax.ShapeDtypeStruct((rows, n), F32),
        scratch_shapes=[pltpu.VMEM((bm, d), BF16)],
        compiler_params=_params(("parallel", "arbitrary")),
        name="norm_proj",
    )(h, gain, w)


def _forget(f_pre, lb):
    span = 1.0 - lb
    w = span * _sigmoid(f_pre)
    log2_f = jnp.log2(jnp.maximum(lb + w, FORGET_FLOOR))
    return log2_f, span - w


def _tri(lower):
    r = lax.broadcasted_iota(jnp.int32, (CHUNK, CHUNK), 0)
    c = lax.broadcasted_iota(jnp.int32, (CHUNK, CHUNK), 1)
    return (r >= c) if lower else (r <= c)


def _cumsum_rows(tri_bf16, g):
    hi = g.astype(BF16)
    lo = (g - hi.astype(F32)).astype(BF16)
    return _dot(tri_bf16, hi) + _dot(tri_bf16, lo)


def _pad_row_mask(is_first):
    r = lax.broadcasted_iota(jnp.int32, (CHUNK, 1), 0)
    return jnp.logical_or(jnp.logical_not(is_first), r >= META_PAD)


def _bwd_state_kernel(fb_ref, ip_ref, lb_ref, sb_ref, st_scr, *, lay, cpt, heads):
    tile = pl.num_programs(0) - 1 - pl.program_id(0)
    upper = jnp.where(_tri(False), 1.0, 0.0).astype(BF16)
    lb = lb_ref[...]

    @pl.when(pl.program_id(0) == 0)
    def _():
        st_scr[...] = jnp.zeros_like(st_scr)

    group = 2 if heads % 2 == 0 else 1
    slabs = [(c, g) for c in reversed(range(cpt)) for g in range(heads // group)]

    def stage_gates(c, g):
        idx, _ = _chunk_in_seq(tile * cpt + c, lay)
        rows = slice(c * CHUNK, (c + 1) * CHUNK)
        cols = slice(g * group * HEAD, (g + 1) * group * HEAD)
        log2_f, k = _forget(fb_ref[rows, cols], lb[:, cols])
        return dict(c=c, g=g, is_first=idx == 0, k=k,
                    v16=ip_ref[rows, cols].astype(BF16),
                    cb=_cumsum_rows(upper, log2_f))

    def stage_operands(x):
        c_end = jnp.where(x["is_first"], CLEAR_LOG2_DECAY, x["cb"][0:1, :])
        x["khat"] = (x["k"] * jnp.exp2(c_end - x["cb"])).astype(BF16)
        x["decay"] = jnp.exp2(c_end)

    def stage_update(x):
        for j in range(group):
            h = x["g"] * group + j
            hs = slice(h * HEAD, (h + 1) * HEAD)
            hl = slice(j * HEAD, (j + 1) * HEAD)
            st_old = st_scr[:, hs]
            sb_ref[x["c"], :, hs] = st_old.astype(BF16)
            st_scr[:, hs] = st_old * x["decay"][:, hl] + _dot_tn(x["v16"][:, hl], x["khat"][:, hl])

    stages = (stage_operands, stage_update)
    in_flight = []
    for step in range(len(slabs) + len(stages)):
        in_flight.insert(0, stage_gates(*slabs[step]) if step < len(slabs) else None)
        in_flight = in_flight[:len(stages) + 1]
        for depth, stage in enumerate(stages, start=1):
            if depth < len(in_flight) and in_flight[depth] is not None:
                stage(in_flight[depth])


def _bwd_states(u, lb_b, *, lay, cpt, hgw):
    nc = _num_chunks(lay)
    nt = nc // cpt
    tr = cpt * CHUNK
    heads = hgw // HEAD
    kern = functools.partial(_bwd_state_kernel, lay=lay, cpt=cpt, heads=heads)
    return pl.pallas_call(
        kern,
        grid=(nt,),
        in_specs=[
            pl.BlockSpec((tr, hgw), lambda j: (nt - 1 - j, 2)),
            pl.BlockSpec((tr, hgw), lambda j: (nt - 1 - j, 3)),
            pl.BlockSpec((1, hgw), lambda j: (0, 0)),
        ],
        out_specs=pl.BlockSpec((cpt, HEAD, hgw), lambda j: (nt - 1 - j, 0, 0)),
        out_shape=jax.ShapeDtypeStruct((nc, HEAD, hgw), BF16),
        scratch_shapes=[pltpu.VMEM((HEAD, hgw), F32)],
        compiler_params=_params(("arbitrary",)),
        name="hgrn2_bwd_states",
    )(u, u, lb_b)


def _chunk_gates(q_ref, ff_ref, fb_ref, ip_ref, rows, cols, lb_f, lb_b, lower, upper):
    q = _silu(q_ref[rows, cols])
    v16 = ip_ref[rows, cols].astype(BF16)
    gf, kf = _forget(ff_ref[rows, cols], lb_f[:, cols])
    gb, kb = _forget(fb_ref[rows, cols], lb_b[:, cols])
    bf = _cumsum_rows(lower, gf)
    cb = _cumsum_rows(upper, gb)
    return q, v16, kf, bf, kb, cb


def _head_output(o, gate, head_norm, valid, dtype):
    ms = jnp.mean(o * o, axis=-1, keepdims=True)
    y = o * lax.rsqrt(ms + EPS) * head_norm * gate
    return jnp.where(valid, y, 0.0).astype(dtype)


def _exact_scores(q_scr, k_scr, b_scr, hs, mask):
    qh = q_scr[:, hs]
    bh = b_scr[:, hs]
    lane = lax.broadcasted_iota(jnp.int32, (CHUNK, CHUNK), 1)

    def body(sg, acc):
        group = pl.ds(pl.multiple_of(sg * SUBLANES, SUBLANES), SUBLANES)
        k8 = k_scr[group, hs]
        b8 = b_scr[group, hs]
        for j in range(SUBLANES):
            e = jnp.exp2(jnp.minimum(bh - b8[j:j + 1, :], 0.0))
            col = jnp.sum(qh * k8[j:j + 1, :] * e, axis=1, keepdims=True)
            acc = jnp.where(lane == sg * SUBLANES + j, col, acc)
        return acc

    acc = lax.fori_loop(0, CHUNK // SUBLANES, body, jnp.zeros((CHUNK, CHUNK), F32))
    return jnp.where(mask, acc, 0.0)


def _hgrn2_kernel(q_ref, ff_ref, fb_ref, ip_ref, gt_ref, sb_ref, lb_ref, hn_ref, o_ref,
                  st_scr, oi_scr, redo_ref, q_scr, kf_scr, bf_scr, kb_scr, cb_scr,
                  *, lay, cpt, heads):
    tile = pl.program_id(0)
    lower_m = _tri(True)
    upper_m = _tri(False)
    lower = jnp.where(lower_m, 1.0, 0.0).astype(BF16)
    upper = jnp.where(upper_m, 1.0, 0.0).astype(BF16)
    lb_f = lb_ref[0:1, :]
    lb_b = lb_ref[1:2, :]
    hn = hn_ref[...]
    half = CHUNK // 2

    @pl.when(tile == 0)
    def _():
        st_scr[...] = jnp.zeros_like(st_scr)

    group = 2 if heads % 2 == 0 else 1
    slabs = [(c, g) for c in range(cpt) for g in range(heads // group)]
    margin = {}

    def stage_gates(c, g):
        idx, cps = _chunk_in_seq(tile * cpt + c, lay)
        rows = slice(c * CHUNK, (c + 1) * CHUNK)
        cols = slice(g * group * HEAD, (g + 1) * group * HEAD)
        q, v16, kf, bf, kb, cb = _chunk_gates(q_ref, ff_ref, fb_ref, ip_ref, rows, cols,
                                              lb_f, lb_b, lower, upper)
        return dict(c=c, g=g, rows=rows, valid=_pad_row_mask(idx == 0), is_last=idx == cps - 1,
                    q=q, v16=v16, kf=kf, bf=bf, kb=kb, cb=cb, gate=_silu(gt_ref[rows, cols]))

    def stage_operands(x):
        q, kf, bf, kb, cb = x["q"], x["kf"], x["bf"], x["kb"], x["cb"]
        b_last = bf[CHUNK - 1:CHUNK, :]
        c_first = cb[0:1, :]
        rf = bf[half - 1:half, :]
        rb = cb[half:half + 1, :]
        ef = jnp.exp2(bf - rf)
        eb = jnp.exp2(cb - rb)
        b_end = jnp.where(x["is_last"], CLEAR_LOG2_DECAY, b_last)
        m = jnp.minimum(jnp.minimum(rf, b_last - rf), jnp.minimum(rb, c_first - rb))
        margin[x["c"]] = m if x["g"] == 0 else jnp.minimum(margin[x["c"]], m)
        if x["g"] == heads // group - 1:
            redo_ref[x["c"]] = jnp.where(jnp.min(margin[x["c"]]) < -SAFE_LOG2_DECAY, 1, 0)
        x.update(
            qtf=(q * ef).astype(BF16),
            ktf=(kf * (1.0 / ef)).astype(BF16),
            qtb=(q * eb).astype(BF16),
            ktb=(kb * (1.0 / eb)).astype(BF16),
            qhf=(q * jnp.exp2(bf)).astype(BF16),
            qhb=(q * jnp.exp2(cb)).astype(BF16),
            khat=(kf * jnp.exp2(b_end - bf)).astype(BF16),
            decay=jnp.exp2(b_end),
        )

    def group_heads(x):
        for j in range(group):
            h = x["g"] * group + j
            yield j, slice(h * HEAD, (h + 1) * HEAD), slice(j * HEAD, (j + 1) * HEAD)

    def stage_scores(x):
        x["af"], x["ab"], x["oi"] = {}, {}, {}
        for j, hs, hl in group_heads(x):
            st_old = st_scr[:, hs]
            x["af"][j] = _dot_nt(x["qtf"][:, hl], x["ktf"][:, hl])
            x["ab"][j] = _dot_nt(x["qtb"][:, hl], x["ktb"][:, hl])
            qcat = jnp.concatenate([x["qhf"][:, hl], x["qhb"][:, hl]], axis=1)
            scat = jnp.concatenate([st_old.astype(BF16), sb_ref[x["c"], :, hs]], axis=1)
            x["oi"][j] = _dot_nt(qcat, scat)
            st_scr[:, hs] = st_old * x["decay"][:, hl] + _dot_tn(x["v16"][:, hl], x["khat"][:, hl])

    def stage_mix(x):
        x["o"] = {}
        for j, hs, hl in group_heads(x):
            a = (jnp.where(lower_m, x["af"][j], 0.0) + jnp.where(upper_m, x["ab"][j], 0.0)).astype(BF16)
            oi_scr[x["rows"], hs] = x["oi"][j]
            x["o"][j] = _dot(a, x["v16"][:, hl]) + x["oi"][j]

    def stage_output(x):
        for j, hs, hl in group_heads(x):
            o_ref[x["rows"], hs] = _head_output(x["o"][j], x["gate"][:, hl], hn[:, hs], x["valid"],
                                                o_ref.dtype)

    def chunk_redo(c):
        idx, _ = _chunk_in_seq(tile * cpt + c, lay)
        rows = pl.ds(pl.multiple_of(c * CHUNK, CHUNK), CHUNK)
        valid = _pad_row_mask(idx == 0)
        q, v16, kf, bf, kb, cb = _chunk_gates(q_ref, ff_ref, fb_ref, ip_ref, rows, slice(None),
                                              lb_f, lb_b, lower, upper)
        q_scr[...] = q
        kf_scr[...] = kf
        bf_scr[...] = bf
        kb_scr[...] = kb
        cb_scr[...] = cb
        gate = _silu(gt_ref[rows, :])
        for h in range(heads):
            hs = slice(h * HEAD, (h + 1) * HEAD)
            a = (_exact_scores(q_scr, kf_scr, bf_scr, hs, lower_m)
                 + _exact_scores(q_scr, kb_scr, cb_scr, hs, upper_m)).astype(BF16)
            o = _dot(a, v16[:, hs]) + oi_scr[rows, hs]
            o_ref[rows, hs] = _head_output(o, gate[:, hs], hn[:, hs], valid, o_ref.dtype)

    stages = (stage_operands, stage_scores, stage_mix, stage_output)
    in_flight = []
    for step in range(len(slabs) + len(stages)):
        in_flight.insert(0, stage_gates(*slabs[step]) if step < len(slabs) else None)
        in_flight = in_flight[:len(stages) + 1]
        for depth, stage in enumerate(stages, start=1):
            if depth < len(in_flight) and in_flight[depth] is not None:
                stage(in_flight[depth])

    def redo_body(c, carry):
        @pl.when(redo_ref[c] != 0)
        def _():
            chunk_redo(c)
        return carry

    lax.fori_loop(0, cpt, redo_body, 0)


def _hgrn2(u, sb, lb, head_norm, *, lay, cpt, hgw):
    nc = _num_chunks(lay)
    nt = nc // cpt
    tr = cpt * CHUNK
    heads = hgw // HEAD
    kern = functools.partial(_hgrn2_kernel, lay=lay, cpt=cpt, heads=heads)
    col = lambda part: pl.BlockSpec((tr, hgw), lambda i: (i, part))
    chunk_f32 = pltpu.VMEM((CHUNK, hgw), F32)
    return pl.pallas_call(
        kern,
        grid=(nt,),
        in_specs=[
            col(0), col(1), col(2), col(3), col(4),
            pl.BlockSpec((cpt, HEAD, hgw), lambda i: (i, 0, 0)),
            pl.BlockSpec((2, hgw), lambda i: (0, 0)),
            pl.BlockSpec((1, hgw), lambda i: (0, 0)),
        ],
        out_specs=pl.BlockSpec((tr, hgw), lambda i: (i, 0)),
        out_shape=jax.ShapeDtypeStruct((nc * CHUNK, hgw), BF16),
        scratch_shapes=[
            pltpu.VMEM((HEAD, hgw), F32),
            pltpu.VMEM((tr, hgw), F32),
            pltpu.SMEM((cpt,), jnp.int32),
            chunk_f32, chunk_f32, chunk_f32, chunk_f32, chunk_f32,
        ],
        compiler_params=_params(("arbitrary",)),
        name="hgrn2_scan",
    )(u, u, u, u, u, sb, lb, head_norm)


def _pool_bands(tm):
    r = np.arange(tm)[:, None]
    j = np.arange(tm + 2 * POOL_HALO)[None, :] - POOL_HALO
    bands = [(j >= r - w // 2) & (j < r + w - w // 2) for w in POOL_WINDOWS]
    return jnp.asarray(np.stack(bands), BF16)


def _pool_kernel(prev_ref, x_ref, next_ref, band_ref, wp_ref, ps_ref, o_ref, *, lay, tm, sub):
    gw = x_ref.shape[1] // len(POOL_WINDOWS)
    groups = range(len(POOL_WINDOWS))
    cols = [slice(g * gw, (g + 1) * gw) for g in groups]

    def masked_rows(t):
        r0 = t * tm
        idx, cps = _chunk_in_seq((pl.program_id(0) * sub + t) * (tm // CHUNK), lay)
        p0 = idx * CHUNK
        seq_rows = cps * CHUNK
        pos_h = p0 - POOL_HALO + lax.broadcasted_iota(jnp.int32, (tm + 2 * POOL_HALO, 1), 0)
        valid_h = jnp.logical_and(pos_h >= META_PAD, pos_h < seq_rows)
        before = prev_ref[...] if t == 0 else x_ref[r0 - POOL_HALO:r0, :]
        after = next_ref[...] if t == sub - 1 else x_ref[r0 + tm:r0 + tm + POOL_HALO, :]
        x_all = jnp.concatenate([before, x_ref[r0:r0 + tm, :], after], axis=0)
        xm = jnp.where(valid_h, x_all, 0.0)
        hi = xm.astype(BF16)
        lo = (xm - hi.astype(F32)).astype(BF16)
        return dict(r0=r0, pos=p0 + lax.broadcasted_iota(jnp.int32, (tm, 1), 0), seq_rows=seq_rows,
                    xm=xm, hi=hi, lo=lo)

    tiles = [masked_rows(t) for t in range(sub)]
    for x in tiles:
        x["total"] = [_dot(band_ref[g], x["hi"][:, cols[g]]) + _dot(band_ref[g], x["lo"][:, cols[g]])
                      for g in groups]
    for x in tiles:
        x["pooled"] = []
        for g, window in enumerate(POOL_WINDOWS):
            back = window // 2
            count = (jnp.minimum(x["pos"] + (window - back), x["seq_rows"])
                     - jnp.maximum(x["pos"] - back, META_PAD))
            count = jnp.maximum(count, 1).astype(F32)
            centre = x["xm"][POOL_HALO:POOL_HALO + tm, cols[g]]
            x["pooled"].append((x["total"][g] / count - centre).astype(BF16))
    for x in tiles:
        x["y"] = [_dot(x["pooled"][g], wp_ref[g]) for g in groups]
    for x in tiles:
        valid = x["pos"] >= META_PAD
        for g in groups:
            y = jnp.where(valid, x["y"][g] * ps_ref[:, cols[g]], 0.0)
            o_ref[x["r0"]:x["r0"] + tm, cols[g]] = y.astype(o_ref.dtype)


def _pool(u, w_pool, layer, pool_scale, *, lay, tm, sub, hgw):
    rows = u.shape[0]
    pw = pool_scale.shape[1]
    col = 5 * hgw // pw
    bt = tm * sub
    hb = bt // POOL_HALO
    last_hb = rows // POOL_HALO - 1
    bands = _pool_bands(tm)
    kern = functools.partial(_pool_kernel, lay=lay, tm=tm, sub=sub)
    return pl.pallas_call(
        kern,
        grid=(rows // bt,),
        in_specs=[
            pl.BlockSpec((POOL_HALO, pw), lambda i: (jnp.maximum(i * hb - 1, 0), col)),
            pl.BlockSpec((bt, pw), lambda i: (i, col)),
            pl.BlockSpec((POOL_HALO, pw), lambda i: (jnp.minimum((i + 1) * hb, last_hb), col)),
            pl.BlockSpec(bands.shape, lambda i: (0, 0, 0)),
            pl.BlockSpec((None,) + w_pool.shape[1:], lambda i: (layer, 0, 0, 0)),
            pl.BlockSpec((1, pw), lambda i: (0, 0)),
        ],
        out_specs=pl.BlockSpec((bt, pw), lambda i: (i, 0)),
        out_shape=jax.ShapeDtypeStruct((rows, pw), BF16),
        compiler_params=_params(("parallel",)),
        name="pool_mixer",
    )(u, u, u, bands, w_pool, pool_scale)


def _out_proj_kernel(h_ref, yh_ref, yp_ref, w_ref, o_ref):
    hgw = yh_ref.shape[1]
    o_ref[...] = (h_ref[...] + _dot(yh_ref[...], w_ref[0:hgw, :])
                  + _dot(yp_ref[...], w_ref[hgw:, :]))


def _out_proj(h, y_hg, y_pool, w_out, layer, *, bm):
    rows, d = h.shape
    hgw, pw = y_hg.shape[1], y_pool.shape[1]
    return pl.pallas_call(
        _out_proj_kernel,
        grid=(rows // bm,),
        in_specs=[
            pl.BlockSpec((bm, d), lambda i: (i, 0)),
            pl.BlockSpec((bm, hgw), lambda i: (i, 0)),
            pl.BlockSpec((bm, pw), lambda i: (i, 0)),
            pl.BlockSpec((None,) + w_out.shape[1:], lambda i: (layer, 0, 0)),
        ],
        out_specs=pl.BlockSpec((bm, d), lambda i: (i, 0)),
        out_shape=jax.ShapeDtypeStruct((rows, d), F32),
        compiler_params=_params(("parallel",)),
        name="out_proj",
    )(h, y_hg, y_pool, w_out)


def _mlp_kernel(h_ref, g_ref, wu_ref, wd_ref, *rest, final, f_axis):
    o_ref, m_scr = rest[-2:]
    f = pl.program_id(f_axis)

    def contribution(m):
        hidden = jnp.square(jnp.maximum(_dot(m, wu_ref[...]), 0.0)).astype(BF16)
        return _dot(hidden, wd_ref[...])

    @pl.when(f == 0)
    def _():
        h = h_ref[...]
        m = _rms(h, g_ref[...]).astype(BF16)
        m_scr[...] = m
        o_ref[...] = h + contribution(m)

    @pl.when(f > 0)
    def _():
        o_ref[...] += contribution(m_scr[...])

    if final:
        @pl.when(f == pl.num_programs(f_axis) - 1)
        def _():
            o_ref[...] = _rms(o_ref[...], rest[0][...])


def _mlp(h, gain, w_up, w_down, layer, *, bm, bf):
    rows, d = h.shape
    dff = w_up.shape[2]
    return pl.pallas_call(
        functools.partial(_mlp_kernel, final=False, f_axis=1),
        grid=(rows // bm, dff // bf),
        in_specs=[
            pl.BlockSpec((bm, d), lambda i, f: (i, 0)),
            pl.BlockSpec((1, d), lambda i, f: (0, 0)),
            pl.BlockSpec((None, d, bf), lambda i, f: (layer, 0, f)),
            pl.BlockSpec((None, bf, d), lambda i, f: (layer, f, 0)),
        ],
        out_specs=pl.BlockSpec((bm, d), lambda i, f: (i, 0)),
        out_shape=jax.ShapeDtypeStruct((rows, d), F32),
        scratch_shapes=[pltpu.VMEM((bm, d), BF16)],
        compiler_params=_params(("parallel", "arbitrary")),
        name="mlp",
    )(h, gain, w_up, w_down)


def _mlp_final(h, gain, w_up, w_down, layer, final_gain, *, row0, n_seq, seq_rows, s, bm, bf):
    d = h.shape[1]
    dff = w_up.shape[2]
    skip = seq_rows - s
    return pl.pallas_call(
        functools.partial(_mlp_kernel, final=True, f_axis=2),
        grid=(n_seq, s // bm, dff // bf),
        in_specs=[
            pl.BlockSpec((pl.Element(bm), pl.Element(d)),
                         lambda b, i, f: (pl.multiple_of(row0 + b * seq_rows + skip + i * bm, CHUNK), 0)),
            pl.BlockSpec((1, d), lambda b, i, f: (0, 0)),
            pl.BlockSpec((None, d, bf), lambda b, i, f: (layer, 0, f)),
            pl.BlockSpec((None, bf, d), lambda b, i, f: (layer, f, 0)),
            pl.BlockSpec((1, d), lambda b, i, f: (0, 0)),
        ],
        out_specs=pl.BlockSpec((None, bm, d), lambda b, i, f: (b, i, 0)),
        out_shape=jax.ShapeDtypeStruct((n_seq, s, d), F32),
        scratch_shapes=[pltpu.VMEM((bm, d), BF16)],
        compiler_params=_params(("parallel", "parallel", "arbitrary")),
        name="mlp_final",
    )(h, gain, w_up, w_down, final_gain)


def _pack_rows(xs, meta):
    d = meta.shape[1]
    lead = jnp.concatenate([jnp.zeros((META_PAD, d), F32), meta.astype(F32)], axis=0)
    pieces = []
    for x in xs:
        for b in range(x.shape[0]):
            pieces += [lead, x[b]]
    return jnp.concatenate(pieces, axis=0)


def _tile_plan(lay):
    nc = _num_chunks(lay)
    g = 1
    for d in range(1, min(lay.cpa, lay.cpb) + 1):
        if lay.cpa % d == 0 and lay.cpb % d == 0:
            g = d
    return dict(
        proj_bm=CHUNK * _largest_divisor(nc, 20),
        scan_cpt=_largest_divisor(nc, 12),
        pool_tm=CHUNK * _largest_divisor(g, 4),
        pool_sub=_largest_divisor(nc // _largest_divisor(g, 4), 4),
        out_bm=CHUNK * _largest_divisor(nc, 10),
        mlp_bm=CHUNK * _largest_divisor(nc, 10),
    )


def kernel(x_prompt, x_sample, meta_tokens, w_in, w_pool, pool_scale, hg_lower_bound, hg_head_norm,
           w_out, norm_mix, norm_mlp, w_up, w_down, final_norm):
    depth, d, in_cols = w_in.shape
    pw = pool_scale.shape[1]
    hgw = hg_head_norm.shape[1]
    dff = w_up.shape[2]
    lead = N_META + META_PAD
    s_a, s_b = x_prompt.shape[1], x_sample.shape[1]
    assert in_cols == 5 * hgw + pw and hgw % HEAD == 0 and pw == hgw
    assert s_a % CHUNK == 0 and s_b % CHUNK == 0
    lay = Layout(x_prompt.shape[0], (s_a + lead) // CHUNK, x_sample.shape[0], (s_b + lead) // CHUNK)
    plan = _tile_plan(lay)
    proj_bn = in_cols // _largest_divisor(in_cols // 256, 6)
    mlp_bf = dff // _largest_divisor(dff // 256, 8)

    probs = jax.nn.softmax(hg_lower_bound.astype(F32), axis=1)
    lower = jnp.cumsum(probs, axis=1) - probs[:, :1]

    h = _pack_rows((x_prompt, x_sample), meta_tokens)
    row = lambda a: a.astype(F32).reshape(1, -1)
    w_in, w_pool, w_out, w_up, w_down = (w.astype(BF16) for w in (w_in, w_pool, w_out, w_up, w_down))
    for l in range(depth):
        u = _norm_proj(h, row(norm_mix[l]), w_in, l, bm=plan["proj_bm"], bn=proj_bn)
        sb = _bwd_states(u, lower[1:2, l], lay=lay, cpt=plan["scan_cpt"], hgw=hgw)
        y_hg = _hgrn2(u, sb, lower[:, l], row(hg_head_norm[l]), lay=lay, cpt=plan["scan_cpt"], hgw=hgw)
        y_pool = _pool(u, w_pool, l, row(pool_scale[l]), lay=lay, tm=plan["pool_tm"],
                       sub=plan["pool_sub"], hgw=hgw)
        h = _out_proj(h, y_hg, y_pool, w_out, l, bm=plan["out_bm"])
        mlp_w = (row(norm_mlp[l]), w_up, w_down, l)
        if l < depth - 1:
            h = _mlp(h, *mlp_w, bm=plan["mlp_bm"], bf=mlp_bf)

    def final(row0, n_seq, s):
        bm = CHUNK * _largest_divisor(s // CHUNK, 8)
        return _mlp_final(h, *mlp_w, row(final_norm), row0=row0, n_seq=n_seq, seq_rows=s + lead,
                          s=s, bm=bm, bf=2 * mlp_bf)

    return (final(0, lay.n_a, s_a), final(lay.n_a * lay.cpa * CHUNK, lay.n_b, s_b))
```

```python
import collections
import functools

import jax
import jax.numpy as jnp
import numpy as np
from jax import lax
from jax.experimental import pallas as pl
from jax.experimental.pallas import tpu as pltpu

N_META = 16
CHUNK = 64
META_PAD = (-N_META) % CHUNK
HEAD = 128
SUBLANES = 8
POOL_WINDOWS = (2, 4, 8, 16)
POOL_HALO = 8
OUT_PIECE = 256
EPS = 1e-6
FORGET_FLOOR = 1e-30
SAFE_LOG2_DECAY = 86.0
CLEAR_LOG2_DECAY = -1e30

VMEM_LIMIT_BYTES = 56 * 1024 * 1024

F32 = jnp.float32
BF16 = jnp.bfloat16

Layout = collections.namedtuple("Layout", "n_a cpa n_b cpb")


def _num_chunks(lay):
    return lay.n_a * lay.cpa + lay.n_b * lay.cpb


def _chunk_in_seq(cg, lay):
    na = lay.n_a * lay.cpa
    in_a = cg < na
    idx = jnp.where(in_a, lax.rem(cg, lay.cpa), lax.rem(jnp.maximum(cg - na, 0), lay.cpb))
    cps = jnp.where(in_a, lay.cpa, lay.cpb)
    return idx, cps


def _largest_divisor(n, cap):
    best = 1
    for d in range(1, n + 1):
        if n % d == 0 and d <= cap:
            best = d
    return best


def _rms(x, gain):
    ms = jnp.mean(x * x, axis=-1, keepdims=True)
    return x * lax.rsqrt(ms + EPS) * gain


def _sigmoid(x):
    return 1.0 / (1.0 + jnp.exp(-x))


def _silu(x):
    hx = 0.5 * x
    return hx * jnp.tanh(hx) + hx


def _dot(a, b):
    return jnp.dot(a, b, preferred_element_type=F32)


def _dot_nt(a, b):
    return lax.dot_general(a, b, (((1,), (1,)), ((), ())), preferred_element_type=F32)


def _dot_tn(a, b):
    return lax.dot_general(a, b, (((0,), (0,)), ((), ())), preferred_element_type=F32)


def _params(sem):
    return pltpu.CompilerParams(dimension_semantics=sem, vmem_limit_bytes=VMEM_LIMIT_BYTES)


def _norm_proj_kernel(h_ref, g_ref, w_ref, o_ref, a_scr):
    @pl.when(pl.program_id(1) == 0)
    def _():
        a = _rms(h_ref[...], g_ref[...]).astype(BF16)
        a_scr[...] = a
        o_ref[...] = _dot(a, w_ref[...])

    @pl.when(pl.program_id(1) > 0)
    def _():
        o_ref[...] = _dot(a_scr[...], w_ref[...])


def _norm_proj(h, gain, w, layer, *, bm, bn):
    rows, d = h.shape
    n = w.shape[2]
    return pl.pallas_call(
        _norm_proj_kernel,
        grid=(rows // bm, n // bn),
        in_specs=[
            pl.BlockSpec((bm, d), lambda i, j: (i, 0)),
            pl.BlockSpec((1, d), lambda i, j: (0, 0)),
            pl.BlockSpec((None, d, bn), lambda i, j: (layer, 0, j)),
        ],
        out_specs=pl.BlockSpec((bm, bn), lambda i, j: (i, j)),
        out_shape=jax.ShapeDtypeStruct((rows, n), F32),
        scratch_shapes=[pltpu.VMEM((bm, d), BF16)],
        compiler_params=_params(("parallel", "arbitrary")),
        name="norm_proj",
    )(h, gain, w)


def _forget(f_pre, lb):
    span = 1.0 - lb
    w = span * _sigmoid(f_pre)
    log2_f = jnp.log2(jnp.maximum(lb + w, FORGET_FLOOR))
    return log2_f, span - w


def _tri(lower):
    r = lax.broadcasted_iota(jnp.int32, (CHUNK, CHUNK), 0)
    c = lax.broadcasted_iota(jnp.int32, (CHUNK, CHUNK), 1)
    return (r >= c) if lower else (r <= c)


def _cumsum_rows(tri_bf16, g):
    hi = g.astype(BF16)
    lo = (g - hi.astype(F32)).astype(BF16)
    return _dot(tri_bf16, hi) + _dot(tri_bf16, lo)


def _pad_row_mask(is_first):
    r = lax.broadcasted_iota(jnp.int32, (CHUNK, 1), 0)
    return jnp.logical_or(jnp.logical_not(is_first), r >= META_PAD)


def _bwd_state_kernel(fb_ref, ip_ref, lb_ref, sb_ref, st_scr, *, lay, cpt, heads):
    tile = pl.num_programs(0) - 1 - pl.program_id(0)
    upper = jnp.where(_tri(False), 1.0, 0.0).astype(BF16)
    lb = lb_ref[...]

    @pl.when(pl.program_id(0) == 0)
    def _():
        st_scr[...] = jnp.zeros_like(st_scr)

    group = 2 if heads % 2 == 0 else 1
    slabs = [(c, g) for c in reversed(range(cpt)) for g in range(heads // group)]

    def stage_gates(c, g):
        idx, _ = _chunk_in_seq(tile * cpt + c, lay)
        rows = slice(c * CHUNK, (c + 1) * CHUNK)
        cols = slice(g * group * HEAD, (g + 1) * group * HEAD)
        log2_f, k = _forget(fb_ref[rows, cols], lb[:, cols])
        return dict(c=c, g=g, is_first=idx == 0, k=k,
                    v16=ip_ref[rows, cols].astype(BF16),
                    cb=_cumsum_rows(upper, log2_f))

    def stage_operands(x):
        c_end = jnp.where(x["is_first"], CLEAR_LOG2_DECAY, x["cb"][0:1, :])
        x["khat"] = (x["k"] * jnp.exp2(c_end - x["cb"])).astype(BF16)
        x["decay"] = jnp.exp2(c_end)

    def stage_update(x):
        for j in range(group):
            h = x["g"] * group + j
            hs = slice(h * HEAD, (h + 1) * HEAD)
            hl = slice(j * HEAD, (j + 1) * HEAD)
            st_old = st_scr[:, hs]
            sb_ref[x["c"], :, hs] = st_old.astype(BF16)
            st_scr[:, hs] = st_old * x["decay"][:, hl] + _dot_tn(x["v16"][:, hl], x["khat"][:, hl])

    stages = (stage_operands, stage_update)
    in_flight = []
    for step in range(len(slabs) + len(stages)):
        in_flight.insert(0, stage_gates(*slabs[step]) if step < len(slabs) else None)
        in_flight = in_flight[:len(stages) + 1]
        for depth, stage in enumerate(stages, start=1):
            if depth < len(in_flight) and in_flight[depth] is not None:
                stage(in_flight[depth])


def _bwd_states(u, lb_b, *, lay, cpt, hgw):
    nc = _num_chunks(lay)
    nt = nc // cpt
    tr = cpt * CHUNK
    heads = hgw // HEAD
    kern = functools.partial(_bwd_state_kernel, lay=lay, cpt=cpt, heads=heads)
    return pl.pallas_call(
        kern,
        grid=(nt,),
        in_specs=[
            pl.BlockSpec((tr, hgw), lambda j: (nt - 1 - j, 2)),
            pl.BlockSpec((tr, hgw), lambda j: (nt - 1 - j, 3)),
            pl.BlockSpec((1, hgw), lambda j: (0, 0)),
        ],
        out_specs=pl.BlockSpec((cpt, HEAD, hgw), lambda j: (nt - 1 - j, 0, 0)),
        out_shape=jax.ShapeDtypeStruct((nc, HEAD, hgw), BF16),
        scratch_shapes=[pltpu.VMEM((HEAD, hgw), F32)],
        compiler_params=_params(("arbitrary",)),
        name="hgrn2_bwd_states",
    )(u, u, lb_b)


def _chunk_gates(q_ref, ff_ref, fb_ref, ip_ref, rows, cols, lb_f, lb_b, lower, upper):
    q = _silu(q_ref[rows, cols])
    v16 = ip_ref[rows, cols].astype(BF16)
    gf, kf = _forget(ff_ref[rows, cols], lb_f[:, cols])
    gb, kb = _forget(fb_ref[rows, cols], lb_b[:, cols])
    bf = _cumsum_rows(lower, gf)
    cb = _cumsum_rows(upper, gb)
    return q, v16, kf, bf, kb, cb


def _head_output(o, gate, head_norm, valid, dtype):
    ms = jnp.mean(o * o, axis=-1, keepdims=True)
    y = o * lax.rsqrt(ms + EPS) * head_norm * gate
    return jnp.where(valid, y, 0.0).astype(dtype)


def _exact_scores(q_scr, k_scr, b_scr, hs, mask):
    qh = q_scr[:, hs]
    bh = b_scr[:, hs]
    lane = lax.broadcasted_iota(jnp.int32, (CHUNK, CHUNK), 1)

    def body(sg, acc):
        group = pl.ds(pl.multiple_of(sg * SUBLANES, SUBLANES), SUBLANES)
        k8 = k_scr[group, hs]
        b8 = b_scr[group, hs]
        for j in range(SUBLANES):
            e = jnp.exp2(jnp.minimum(bh - b8[j:j + 1, :], 0.0))
            col = jnp.sum(qh * k8[j:j + 1, :] * e, axis=1, keepdims=True)
            acc = jnp.where(lane == sg * SUBLANES + j, col, acc)
        return acc

    acc = lax.fori_loop(0, CHUNK // SUBLANES, body, jnp.zeros((CHUNK, CHUNK), F32))
    return jnp.where(mask, acc, 0.0)


def _hgrn2_kernel(q_ref, ff_ref, fb_ref, ip_ref, gt_ref, sb_ref, lb_ref, hn_ref, h_ref, yp_ref, w_ref,
                  h1_ref, st_scr, oi_scr, redo_ref, y_scr, q_scr, kf_scr, bf_scr, kb_scr, cb_scr,
                  *, lay, cpt, heads):
    step = pl.program_id(0)
    tile = jnp.minimum(step, pl.num_programs(0) - 2)
    slot = lax.rem(step, 2)
    hgw = heads * HEAD
    lower_m = _tri(True)
    upper_m = _tri(False)
    lower = jnp.where(lower_m, 1.0, 0.0).astype(BF16)
    upper = jnp.where(upper_m, 1.0, 0.0).astype(BF16)
    lb_f = lb_ref[0:1, :]
    lb_b = lb_ref[1:2, :]
    hn = hn_ref[...]
    half = CHUNK // 2

    @pl.when(step == 0)
    def _():
        st_scr[...] = jnp.zeros_like(st_scr)
        y_scr[1] = jnp.zeros(y_scr.shape[1:], y_scr.dtype)

    group = 2 if heads % 2 == 0 else 1
    slabs = [(c, g) for c in range(cpt) for g in range(heads // group)]
    margin = {}

    def stage_gates(c, g):
        idx, cps = _chunk_in_seq(tile * cpt + c, lay)
        rows = slice(c * CHUNK, (c + 1) * CHUNK)
        cols = slice(g * group * HEAD, (g + 1) * group * HEAD)
        q, v16, kf, bf, kb, cb = _chunk_gates(q_ref, ff_ref, fb_ref, ip_ref, rows, cols,
                                              lb_f, lb_b, lower, upper)
        return dict(c=c, g=g, rows=rows, valid=_pad_row_mask(idx == 0), is_last=idx == cps - 1,
                    q=q, v16=v16, kf=kf, bf=bf, kb=kb, cb=cb, gate=_silu(gt_ref[rows, cols]))

    def stage_operands(x):
        q, kf, bf, kb, cb = x["q"], x["kf"], x["bf"], x["kb"], x["cb"]
        b_last = bf[CHUNK - 1:CHUNK, :]
        c_first = cb[0:1, :]
        rf = bf[half - 1:half, :]
        rb = cb[half:half + 1, :]
        ef = jnp.exp2(bf - rf)
        eb = jnp.exp2(cb - rb)
        b_end = jnp.where(x["is_last"], CLEAR_LOG2_DECAY, b_last)
        m = jnp.minimum(jnp.minimum(rf, b_last - rf), jnp.minimum(rb, c_first - rb))
        margin[x["c"]] = m if x["g"] == 0 else jnp.minimum(margin[x["c"]], m)
        if x["g"] == heads // group - 1:
            redo_ref[x["c"]] = jnp.where(jnp.min(margin[x["c"]]) < -SAFE_LOG2_DECAY, 1, 0)
        x.update(
            qtf=(q * ef).astype(BF16),
            ktf=(kf * (1.0 / ef)).astype(BF16),
            qtb=(q * eb).astype(BF16),
            ktb=(kb * (1.0 / eb)).astype(BF16),
            qhf=(q * jnp.exp2(bf)).astype(BF16),
            qhb=(q * jnp.exp2(cb)).astype(BF16),
            khat=(kf * jnp.exp2(b_end - bf)).astype(BF16),
            decay=jnp.exp2(b_end),
        )

    def group_heads(x):
        for j in range(group):
            h = x["g"] * group + j
            yield j, slice(h * HEAD, (h + 1) * HEAD), slice(j * HEAD, (j + 1) * HEAD)

    def stage_scores(x):
        x["af"], x["ab"], x["oi"] = {}, {}, {}
        for j, hs, hl in group_heads(x):
            st_old = st_scr[:, hs]
            x["af"][j] = _dot_nt(x["qtf"][:, hl], x["ktf"][:, hl])
            x["ab"][j] = _dot_nt(x["qtb"][:, hl], x["ktb"][:, hl])
            qcat = jnp.concatenate([x["qhf"][:, hl], x["qhb"][:, hl]], axis=1)
            scat = jnp.concatenate([st_old.astype(BF16), sb_ref[x["c"], :, hs]], axis=1)
            x["oi"][j] = _dot_nt(qcat, scat)
            st_scr[:, hs] = st_old * x["decay"][:, hl] + _dot_tn(x["v16"][:, hl], x["khat"][:, hl])

    def stage_mix(x):
        x["o"] = {}
        for j, hs, hl in group_heads(x):
            a = (jnp.where(lower_m, x["af"][j], 0.0) + jnp.where(upper_m, x["ab"][j], 0.0)).astype(BF16)
            oi_scr[x["rows"], hs] = x["oi"][j]
            x["o"][j] = _dot(a, x["v16"][:, hl]) + x["oi"][j]

    def stage_output(x):
        for j, hs, hl in group_heads(x):
            y_scr[slot, x["rows"], hs] = _head_output(x["o"][j], x["gate"][:, hl], hn[:, hs], x["valid"],
                                                      y_scr.dtype)

    def chunk_redo(c):
        idx, _ = _chunk_in_seq(tile * cpt + c, lay)
        rows = pl.ds(pl.multiple_of(c * CHUNK, CHUNK), CHUNK)
        valid = _pad_row_mask(idx == 0)
        q, v16, kf, bf, kb, cb = _chunk_gates(q_ref, ff_ref, fb_ref, ip_ref, rows, slice(None),
                                              lb_f, lb_b, lower, upper)
        q_scr[...] = q
        kf_scr[...] = kf
        bf_scr[...] = bf
        kb_scr[...] = kb
        cb_scr[...] = cb
        gate = _silu(gt_ref[rows, :])
        for h in range(heads):
            hs = slice(h * HEAD, (h + 1) * HEAD)
            a = (_exact_scores(q_scr, kf_scr, bf_scr, hs, lower_m)
                 + _exact_scores(q_scr, kb_scr, cb_scr, hs, upper_m)).astype(BF16)
            o = _dot(a, v16[:, hs]) + oi_scr[rows, hs]
            y_scr[slot, rows, hs] = _head_output(o, gate[:, hs], hn[:, hs], valid, y_scr.dtype)

    def project_piece(n):
        cols = slice(n * OUT_PIECE, (n + 1) * OUT_PIECE)
        h1_ref[:, cols] = (h_ref[:, cols] + _dot(y_scr[1 - slot], w_ref[0:hgw, cols])
                           + _dot(yp_ref[...], w_ref[hgw:, cols]))

    pieces = h1_ref.shape[1] // OUT_PIECE
    every = max(1, len(slabs) // pieces)
    stages = (stage_operands, stage_scores, stage_mix, stage_output)
    in_flight = []
    issued = 0
    for k in range(len(slabs) + len(stages)):
        in_flight.insert(0, stage_gates(*slabs[k]) if k < len(slabs) else None)
        in_flight = in_flight[:len(stages) + 1]
        for depth, stage in enumerate(stages, start=1):
            if depth < len(in_flight) and in_flight[depth] is not None:
                stage(in_flight[depth])
        if k % every == every - 1 and issued < pieces:
            project_piece(issued)
            issued += 1
    for n in range(issued, pieces):
        project_piece(n)

    def redo_body(c, carry):
        @pl.when(redo_ref[c] != 0)
        def _():
            chunk_redo(c)
        return carry

    lax.fori_loop(0, cpt, redo_body, 0)


def _hgrn2_out(u, sb, lb, head_norm, h, y_pool, w_out, layer, *, lay, cpt, hgw):
    nc = _num_chunks(lay)
    nt = nc // cpt
    tr = cpt * CHUNK
    heads = hgw // HEAD
    d = h.shape[1]
    pw = y_pool.shape[1]
    kern = functools.partial(_hgrn2_kernel, lay=lay, cpt=cpt, heads=heads)
    scan_tile = lambda i: jnp.minimum(i, nt - 1)
    proj_tile = lambda i: jnp.maximum(i - 1, 0)
    col = lambda part: pl.BlockSpec((tr, hgw), lambda i: (scan_tile(i), part))
    chunk_f32 = pltpu.VMEM((CHUNK, hgw), F32)
    return pl.pallas_call(
        kern,
        grid=(nt + 1,),
        in_specs=[
            col(0), col(1), col(2), col(3), col(4),
            pl.BlockSpec((cpt, HEAD, hgw), lambda i: (scan_tile(i), 0, 0)),
            pl.BlockSpec((2, hgw), lambda i: (0, 0)),
            pl.BlockSpec((1, hgw), lambda i: (0, 0)),
            pl.BlockSpec((tr, d), lambda i: (proj_tile(i), 0)),
            pl.BlockSpec((tr, pw), lambda i: (proj_tile(i), 0)),
            pl.BlockSpec((None,) + w_out.shape[1:], lambda i: (layer, 0, 0), pipeline_mode=pl.Buffered(1)),
        ],
        out_specs=pl.BlockSpec((tr, d), lambda i: (proj_tile(i), 0)),
        out_shape=jax.ShapeDtypeStruct(h.shape, F32),
        scratch_shapes=[
            pltpu.VMEM((HEAD, hgw), F32),
            pltpu.VMEM((tr, hgw), F32),
            pltpu.SMEM((cpt,), jnp.int32),
            pltpu.VMEM((2, tr, hgw), BF16),
            chunk_f32, chunk_f32, chunk_f32, chunk_f32, chunk_f32,
        ],
        compiler_params=_params(("arbitrary",)),
        name="hgrn2_scan_out_proj",
    )(u, u, u, u, u, sb, lb, head_norm, h, y_pool, w_out)


def _pool_bands(tm):
    r = np.arange(tm)[:, None]
    j = np.arange(tm + 2 * POOL_HALO)[None, :] - POOL_HALO
    bands = [(j >= r - w // 2) & (j < r + w - w // 2) for w in POOL_WINDOWS]
    return jnp.asarray(np.stack(bands), BF16)


def _pool_kernel(prev_ref, x_ref, next_ref, band_ref, wp_ref, ps_ref, o_ref, *, lay, tm, sub):
    gw = x_ref.shape[1] // len(POOL_WINDOWS)
    groups = range(len(POOL_WINDOWS))
    cols = [slice(g * gw, (g + 1) * gw) for g in groups]

    def masked_rows(t):
        r0 = t * tm
        idx, cps = _chunk_in_seq((pl.program_id(0) * sub + t) * (tm // CHUNK), lay)
        p0 = idx * CHUNK
        seq_rows = cps * CHUNK
        pos_h = p0 - POOL_HALO + lax.broadcasted_iota(jnp.int32, (tm + 2 * POOL_HALO, 1), 0)
        valid_h = jnp.logical_and(pos_h >= META_PAD, pos_h < seq_rows)
        before = prev_ref[...] if t == 0 else x_ref[r0 - POOL_HALO:r0, :]
        after = next_ref[...] if t == sub - 1 else x_ref[r0 + tm:r0 + tm + POOL_HALO, :]
        x_all = jnp.concatenate([before, x_ref[r0:r0 + tm, :], after], axis=0)
        xm = jnp.where(valid_h, x_all, 0.0)
        hi = xm.astype(BF16)
        lo = (xm - hi.astype(F32)).astype(BF16)
        return dict(r0=r0, pos=p0 + lax.broadcasted_iota(jnp.int32, (tm, 1), 0), seq_rows=seq_rows,
                    xm=xm, hi=hi, lo=lo)

    tiles = [masked_rows(t) for t in range(sub)]
    for x in tiles:
        x["total"] = [_dot(band_ref[g], x["hi"][:, cols[g]]) + _dot(band_ref[g], x["lo"][:, cols[g]])
                      for g in groups]
    for x in tiles:
        x["pooled"] = []
        for g, window in enumerate(POOL_WINDOWS):
            back = window // 2
            count = (jnp.minimum(x["pos"] + (window - back), x["seq_rows"])
                     - jnp.maximum(x["pos"] - back, META_PAD))
            count = jnp.maximum(count, 1).astype(F32)
            centre = x["xm"][POOL_HALO:POOL_HALO + tm, cols[g]]
            x["pooled"].append((x["total"][g] / count - centre).astype(BF16))
    for x in tiles:
        x["y"] = [_dot(x["pooled"][g], wp_ref[g]) for g in groups]
    for x in tiles:
        valid = x["pos"] >= META_PAD
        for g in groups:
            y = jnp.where(valid, x["y"][g] * ps_ref[:, cols[g]], 0.0)
            o_ref[x["r0"]:x["r0"] + tm, cols[g]] = y.astype(o_ref.dtype)


def _pool(u, w_pool, layer, pool_scale, *, lay, tm, sub, hgw):
    rows = u.shape[0]
    pw = pool_scale.shape[1]
    col = 5 * hgw // pw
    bt = tm * sub
    hb = bt // POOL_HALO
    last_hb = rows // POOL_HALO - 1
    bands = _pool_bands(tm)
    kern = functools.partial(_pool_kernel, lay=lay, tm=tm, sub=sub)
    return pl.pallas_call(
        kern,
        grid=(rows // bt,),
        in_specs=[
            pl.BlockSpec((POOL_HALO, pw), lambda i: (jnp.maximum(i * hb - 1, 0), col)),
            pl.BlockSpec((bt, pw), lambda i: (i, col)),
            pl.BlockSpec((POOL_HALO, pw), lambda i: (jnp.minimum((i + 1) * hb, last_hb), col)),
            pl.BlockSpec(bands.shape, lambda i: (0, 0, 0)),
            pl.BlockSpec((None,) + w_pool.shape[1:], lambda i: (layer, 0, 0, 0)),
            pl.BlockSpec((1, pw), lambda i: (0, 0)),
        ],
        out_specs=pl.BlockSpec((bt, pw), lambda i: (i, 0)),
        out_shape=jax.ShapeDtypeStruct((rows, pw), BF16),
        compiler_params=_params(("parallel",)),
        name="pool_mixer",
    )(u, u, u, bands, w_pool, pool_scale)


def _mlp_kernel(h_ref, g_ref, wu_ref, wd_ref, *rest, final, f_axis):
    o_ref, m_scr = rest[-2:]
    f = pl.program_id(f_axis)

    def contribution(m):
        hidden = jnp.square(jnp.maximum(_dot(m, wu_ref[...]), 0.0)).astype(BF16)
        return _dot(hidden, wd_ref[...])

    @pl.when(f == 0)
    def _():
        h = h_ref[...]
        m = _rms(h, g_ref[...]).astype(BF16)
        m_scr[...] = m
        o_ref[...] = h + contribution(m)

    @pl.when(f > 0)
    def _():
        o_ref[...] += contribution(m_scr[...])

    if final:
        @pl.when(f == pl.num_programs(f_axis) - 1)
        def _():
            o_ref[...] = _rms(o_ref[...], rest[0][...])


def _mlp(h, gain, w_up, w_down, layer, *, bm, bf):
    rows, d = h.shape
    dff = w_up.shape[2]
    return pl.pallas_call(
        functools.partial(_mlp_kernel, final=False, f_axis=1),
        grid=(rows // bm, dff // bf),
        in_specs=[
            pl.BlockSpec((bm, d), lambda i, f: (i, 0)),
            pl.BlockSpec((1, d), lambda i, f: (0, 0)),
            pl.BlockSpec((None, d, bf), lambda i, f: (layer, 0, f)),
            pl.BlockSpec((None, bf, d), lambda i, f: (layer, f, 0)),
        ],
        out_specs=pl.BlockSpec((bm, d), lambda i, f: (i, 0)),
        out_shape=jax.ShapeDtypeStruct((rows, d), F32),
        scratch_shapes=[pltpu.VMEM((bm, d), BF16)],
        compiler_params=_params(("parallel", "arbitrary")),
        name="mlp",
    )(h, gain, w_up, w_down)


def _mlp_final(h, gain, w_up, w_down, layer, final_gain, *, row0, n_seq, seq_rows, s, bm, bf):
    d = h.shape[1]
    dff = w_up.shape[2]
    skip = seq_rows - s
    return pl.pallas_call(
        functools.partial(_mlp_kernel, final=True, f_axis=2),
        grid=(n_seq, s // bm, dff // bf),
        in_specs=[
            pl.BlockSpec((pl.Element(bm), pl.Element(d)),
                         lambda b, i, f: (pl.multiple_of(row0 + b * seq_rows + skip + i * bm, CHUNK), 0)),
            pl.BlockSpec((1, d), lambda b, i, f: (0, 0)),
            pl.BlockSpec((None, d, bf), lambda b, i, f: (layer, 0, f)),
            pl.BlockSpec((None, bf, d), lambda b, i, f: (layer, f, 0)),
            pl.BlockSpec((1, d), lambda b, i, f: (0, 0)),
        ],
        out_specs=pl.BlockSpec((None, bm, d), lambda b, i, f: (b, i, 0)),
        out_shape=jax.ShapeDtypeStruct((n_seq, s, d), F32),
        scratch_shapes=[pltpu.VMEM((bm, d), BF16)],
        compiler_params=_params(("parallel", "parallel", "arbitrary")),
        name="mlp_final",
    )(h, gain, w_up, w_down, final_gain)


def _pack_rows(xs, meta):
    d = meta.shape[1]
    lead = jnp.concatenate([jnp.zeros((META_PAD, d), F32), meta.astype(F32)], axis=0)
    pieces = []
    for x in xs:
        for b in range(x.shape[0]):
            pieces += [lead, x[b]]
    return jnp.concatenate(pieces, axis=0)


def _tile_plan(lay):
    nc = _num_chunks(lay)
    g = 1
    for d in range(1, min(lay.cpa, lay.cpb) + 1):
        if lay.cpa % d == 0 and lay.cpb % d == 0:
            g = d
    return dict(
        proj_bm=CHUNK * _largest_divisor(nc, 20),
        bwd_cpt=_largest_divisor(nc, 12),
        scan_cpt=_largest_divisor(nc, 6),
        pool_tm=CHUNK * _largest_divisor(g, 4),
        pool_sub=_largest_divisor(nc // _largest_divisor(g, 4), 4),
        mlp_bm=CHUNK * _largest_divisor(nc, 10),
    )


def kernel(x_prompt, x_sample, meta_tokens, w_in, w_pool, pool_scale, hg_lower_bound, hg_head_norm,
           w_out, norm_mix, norm_mlp, w_up, w_down, final_norm):
    depth, d, in_cols = w_in.shape
    pw = pool_scale.shape[1]
    hgw = hg_head_norm.shape[1]
    dff = w_up.shape[2]
    lead = N_META + META_PAD
    s_a, s_b = x_prompt.shape[1], x_sample.shape[1]
    assert in_cols == 5 * hgw + pw and hgw % HEAD == 0 and pw == hgw
    assert s_a % CHUNK == 0 and s_b % CHUNK == 0
    lay = Layout(x_prompt.shape[0], (s_a + lead) // CHUNK, x_sample.shape[0], (s_b + lead) // CHUNK)
    plan = _tile_plan(lay)
    proj_bn = in_cols // _largest_divisor(in_cols // 256, 6)
    mlp_bf = dff // _largest_divisor(dff // 256, 8)

    probs = jax.nn.softmax(hg_lower_bound.astype(F32), axis=1)
    lower = jnp.cumsum(probs, axis=1) - probs[:, :1]

    h = _pack_rows((x_prompt, x_sample), meta_tokens)
    row = lambda a: a.astype(F32).reshape(1, -1)
    w_in, w_pool, w_out, w_up, w_down = (w.astype(BF16) for w in (w_in, w_pool, w_out, w_up, w_down))
    for l in range(depth):
        u = _norm_proj(h, row(norm_mix[l]), w_in, l, bm=plan["proj_bm"], bn=proj_bn)
        sb = _bwd_states(u, lower[1:2, l], lay=lay, cpt=plan["bwd_cpt"], hgw=hgw)
        y_pool = _pool(u, w_pool, l, row(pool_scale[l]), lay=lay, tm=plan["pool_tm"],
                       sub=plan["pool_sub"], hgw=hgw)
        h = _hgrn2_out(u, sb, lower[:, l], row(hg_head_norm[l]), h, y_pool, w_out, l,
                       lay=lay, cpt=plan["scan_cpt"], hgw=hgw)
        mlp_w = (row(norm_mlp[l]), w_up, w_down, l)
        if l < depth - 1:
            h = _mlp(h, *mlp_w, bm=plan["mlp_bm"], bf=mlp_bf)

    def final(row0, n_seq, s):
        bm = CHUNK * _largest_divisor(s // CHUNK, 8)
        return _mlp_final(h, *mlp_w, row(final_norm), row0=row0, n_seq=n_seq, seq_rows=s + lead,
                          s=s, bm=bm, bf=2 * mlp_bf)

    return (final(0, lay.n_a, s_a), final(lay.n_a * lay.cpa * CHUNK, lay.n_b, s_b))
```

```python
import collections
import functools

import jax
import jax.numpy as jnp
import numpy as np
from jax import lax
from jax.experimental import pallas as pl
from jax.experimental.pallas import tpu as pltpu

N_META = 16
CHUNK = 64
META_PAD = (-N_META) % CHUNK
HEAD = 128
SUBLANES = 8
POOL_WINDOWS = (2, 4, 8, 16)
POOL_HALO = 8
OUT_PIECE = 256
EPS = 1e-6
FORGET_FLOOR = 1e-30
SAFE_LOG2_DECAY = 86.0
CLEAR_LOG2_DECAY = -1e30

VMEM_LIMIT_BYTES = 56 * 1024 * 1024

F32 = jnp.float32
BF16 = jnp.bfloat16

Layout = collections.namedtuple("Layout", "n_a cpa n_b cpb")


def _num_chunks(lay):
    return lay.n_a * lay.cpa + lay.n_b * lay.cpb


def _chunk_in_seq(cg, lay):
    na = lay.n_a * lay.cpa
    in_a = cg < na
    idx = jnp.where(in_a, lax.rem(cg, lay.cpa), lax.rem(jnp.maximum(cg - na, 0), lay.cpb))
    cps = jnp.where(in_a, lay.cpa, lay.cpb)
    return idx, cps


def _largest_divisor(n, cap):
    best = 1
    for d in range(1, n + 1):
        if n % d == 0 and d <= cap:
            best = d
    return best


def _rms(x, gain):
    ms = jnp.mean(x * x, axis=-1, keepdims=True)
    return x * lax.rsqrt(ms + EPS) * gain


def _sigmoid(x):
    return 1.0 / (1.0 + jnp.exp(-x))


def _silu(x):
    hx = 0.5 * x
    return hx * jnp.tanh(hx) + hx


def _dot(a, b):
    return jnp.dot(a, b, preferred_element_type=F32)


def _dot_nt(a, b):
    return lax.dot_general(a, b, (((1,), (1,)), ((), ())), preferred_element_type=F32)


def _dot_tn(a, b):
    return lax.dot_general(a, b, (((0,), (0,)), ((), ())), preferred_element_type=F32)


def _params(sem):
    return pltpu.CompilerParams(dimension_semantics=sem, vmem_limit_bytes=VMEM_LIMIT_BYTES)


def _norm_proj_kernel(h_ref, g_ref, w_ref, o_ref, a_scr):
    @pl.when(pl.program_id(1) == 0)
    def _():
        a = _rms(h_ref[...], g_ref[...]).astype(BF16)
        a_scr[...] = a
        o_ref[...] = _dot(a, w_ref[...])

    @pl.when(pl.program_id(1) > 0)
    def _():
        o_ref[...] = _dot(a_scr[...], w_ref[...])


def _norm_proj(h, gain, w, layer, *, bm, bn):
    rows, d = h.shape
    n = w.shape[2]
    return pl.pallas_call(
        _norm_proj_kernel,
        grid=(rows // bm, n // bn),
        in_specs=[
            pl.BlockSpec((bm, d), lambda i, j: (i, 0)),
            pl.BlockSpec((1, d), lambda i, j: (0, 0)),
            pl.BlockSpec((None, d, bn), lambda i, j: (layer, 0, j)),
        ],
        out_specs=pl.BlockSpec((bm, bn), lambda i, j: (i, j)),
        out_shape=jax.ShapeDtypeStruct((rows, n), F32),
        scratch_shapes=[pltpu.VMEM((bm, d), BF16)],
        compiler_params=_params(("parallel", "arbitrary")),
        name="norm_proj",
    )(h, gain, w)


def _forget(f_pre, lb):
    span = 1.0 - lb
    w = span * _sigmoid(f_pre)
    log2_f = jnp.log2(jnp.maximum(lb + w, FORGET_FLOOR))
    return log2_f, span - w


def _tri(lower):
    r = lax.broadcasted_iota(jnp.int32, (CHUNK, CHUNK), 0)
    c = lax.broadcasted_iota(jnp.int32, (CHUNK, CHUNK), 1)
    return (r >= c) if lower else (r <= c)


def _cumsum_rows(tri_bf16, g):
    hi = g.astype(BF16)
    lo = (g - hi.astype(F32)).astype(BF16)
    return _dot(tri_bf16, hi) + _dot(tri_bf16, lo)


def _pad_row_mask(is_first):
    r = lax.broadcasted_iota(jnp.int32, (CHUNK, 1), 0)
    return jnp.logical_or(jnp.logical_not(is_first), r >= META_PAD)


def _bwd_state_kernel(fb_ref, ip_ref, lb_ref, sb_ref, st_scr, *, lay, cpt, heads):
    tile = pl.num_programs(0) - 1 - pl.program_id(0)
    upper = jnp.where(_tri(False), 1.0, 0.0).astype(BF16)
    lb = lb_ref[...]

    @pl.when(pl.program_id(0) == 0)
    def _():
        st_scr[...] = jnp.zeros_like(st_scr)

    group = 2 if heads % 2 == 0 else 1
    slabs = [(c, g) for c in reversed(range(cpt)) for g in range(heads // group)]

    def stage_gates(c, g):
        idx, _ = _chunk_in_seq(tile * cpt + c, lay)
        rows = slice(c * CHUNK, (c + 1) * CHUNK)
        cols = slice(g * group * HEAD, (g + 1) * group * HEAD)
        log2_f, k = _forget(fb_ref[rows, cols], lb[:, cols])
        return dict(c=c, g=g, is_first=idx == 0, k=k,
                    v16=ip_ref[rows, cols].astype(BF16),
                    cb=_cumsum_rows(upper, log2_f))

    def stage_operands(x):
        c_end = jnp.where(x["is_first"], CLEAR_LOG2_DECAY, x["cb"][0:1, :])
        x["khat"] = (x["k"] * jnp.exp2(c_end - x["cb"])).astype(BF16)
        x["decay"] = jnp.exp2(c_end)

    def stage_update(x):
        for j in range(group):
            h = x["g"] * group + j
            hs = slice(h * HEAD, (h + 1) * HEAD)
            hl = slice(j * HEAD, (j + 1) * HEAD)
            st_old = st_scr[:, hs]
            sb_ref[x["c"], :, hs] = st_old.astype(BF16)
            st_scr[:, hs] = st_old * x["decay"][:, hl] + _dot_tn(x["v16"][:, hl], x["khat"][:, hl])

    stages = (stage_operands, stage_update)
    in_flight = []
    for step in range(len(slabs) + len(stages)):
        in_flight.insert(0, stage_gates(*slabs[step]) if step < len(slabs) else None)
        in_flight = in_flight[:len(stages) + 1]
        for depth, stage in enumerate(stages, start=1):
            if depth < len(in_flight) and in_flight[depth] is not None:
                stage(in_flight[depth])


def _bwd_states(u, lb_b, *, lay, cpt, hgw):
    nc = _num_chunks(lay)
    nt = nc // cpt
    tr = cpt * CHUNK
    heads = hgw // HEAD
    kern = functools.partial(_bwd_state_kernel, lay=lay, cpt=cpt, heads=heads)
    return pl.pallas_call(
        kern,
        grid=(nt,),
        in_specs=[
            pl.BlockSpec((tr, hgw), lambda j: (nt - 1 - j, 2)),
            pl.BlockSpec((tr, hgw), lambda j: (nt - 1 - j, 3)),
            pl.BlockSpec((1, hgw), lambda j: (0, 0)),
        ],
        out_specs=pl.BlockSpec((cpt, HEAD, hgw), lambda j: (nt - 1 - j, 0, 0)),
        out_shape=jax.ShapeDtypeStruct((nc, HEAD, hgw), BF16),
        scratch_shapes=[pltpu.VMEM((HEAD, hgw), F32)],
        compiler_params=_params(("arbitrary",)),
        name="hgrn2_bwd_states",
    )(u, u, lb_b)


def _chunk_gates(q_ref, ff_ref, fb_ref, ip_ref, rows, cols, lb_f, lb_b, lower, upper):
    q = _silu(q_ref[rows, cols])
    v16 = ip_ref[rows, cols].astype(BF16)
    gf, kf = _forget(ff_ref[rows, cols], lb_f[:, cols])
    gb, kb = _forget(fb_ref[rows, cols], lb_b[:, cols])
    bf = _cumsum_rows(lower, gf)
    cb = _cumsum_rows(upper, gb)
    return q, v16, kf, bf, kb, cb


def _head_output(o, gate, head_norm, valid, dtype):
    ms = jnp.mean(o * o, axis=-1, keepdims=True)
    y = o * lax.rsqrt(ms + EPS) * head_norm * gate
    return jnp.where(valid, y, 0.0).astype(dtype)


def _exact_scores(q_scr, k_scr, b_scr, hs, mask):
    qh = q_scr[:, hs]
    bh = b_scr[:, hs]
    lane = lax.broadcasted_iota(jnp.int32, (CHUNK, CHUNK), 1)

    def body(sg, acc):
        group = pl.ds(pl.multiple_of(sg * SUBLANES, SUBLANES), SUBLANES)
        k8 = k_scr[group, hs]
        b8 = b_scr[group, hs]
        for j in range(SUBLANES):
            e = jnp.exp2(jnp.minimum(bh - b8[j:j + 1, :], 0.0))
            col = jnp.sum(qh * k8[j:j + 1, :] * e, axis=1, keepdims=True)
            acc = jnp.where(lane == sg * SUBLANES + j, col, acc)
        return acc

    acc = lax.fori_loop(0, CHUNK // SUBLANES, body, jnp.zeros((CHUNK, CHUNK), F32))
    return jnp.where(mask, acc, 0.0)


def _hgrn2_kernel(q_ref, ff_ref, fb_ref, ip_ref, gt_ref, sb_ref, lb_ref, hn_ref, h_ref, yp_ref, w_ref,
                  h1_ref, st_scr, oi_scr, redo_ref, y_scr, q_scr, kf_scr, bf_scr, kb_scr, cb_scr,
                  *, lay, cpt, heads):
    step = pl.program_id(0)
    tile = jnp.minimum(step, pl.num_programs(0) - 2)
    slot = lax.rem(step, 2)
    hgw = heads * HEAD
    lower_m = _tri(True)
    upper_m = _tri(False)
    lower = jnp.where(lower_m, 1.0, 0.0).astype(BF16)
    upper = jnp.where(upper_m, 1.0, 0.0).astype(BF16)
    lb_f = lb_ref[0:1, :]
    lb_b = lb_ref[1:2, :]
    hn = hn_ref[...]
    half = CHUNK // 2

    @pl.when(step == 0)
    def _():
        st_scr[...] = jnp.zeros_like(st_scr)
        y_scr[1] = jnp.zeros(y_scr.shape[1:], y_scr.dtype)

    group = 2 if heads % 2 == 0 else 1
    slabs = [(c, g) for c in range(cpt) for g in range(heads // group)]
    margin = {}

    def stage_gates(c, g):
        idx, cps = _chunk_in_seq(tile * cpt + c, lay)
        rows = slice(c * CHUNK, (c + 1) * CHUNK)
        cols = slice(g * group * HEAD, (g + 1) * group * HEAD)
        q, v16, kf, bf, kb, cb = _chunk_gates(q_ref, ff_ref, fb_ref, ip_ref, rows, cols,
                                              lb_f, lb_b, lower, upper)
        return dict(c=c, g=g, rows=rows, valid=_pad_row_mask(idx == 0), is_last=idx == cps - 1,
                    q=q, v16=v16, kf=kf, bf=bf, kb=kb, cb=cb, gate=_silu(gt_ref[rows, cols]))

    def stage_operands(x):
        q, kf, bf, kb, cb = x["q"], x["kf"], x["bf"], x["kb"], x["cb"]
        b_last = bf[CHUNK - 1:CHUNK, :]
        c_first = cb[0:1, :]
        rf = bf[half - 1:half, :]
        rb = cb[half:half + 1, :]
        ef = jnp.exp2(bf - rf)
        eb = jnp.exp2(cb - rb)
        b_end = jnp.where(x["is_last"], CLEAR_LOG2_DECAY, b_last)
        q_big = jnp.log2(jnp.maximum(jnp.max(jnp.abs(q), axis=0, keepdims=True), 1.0))
        m = jnp.minimum(jnp.minimum(rf - q_big, b_last - rf), jnp.minimum(rb - q_big, c_first - rb))
        margin[x["c"]] = m if x["g"] == 0 else jnp.minimum(margin[x["c"]], m)
        if x["g"] == heads // group - 1:
            redo_ref[x["c"]] = jnp.where(jnp.min(margin[x["c"]]) < -SAFE_LOG2_DECAY, 1, 0)
        x.update(
            qtf=(q * ef).astype(BF16),
            ktf=(kf * (1.0 / ef)).astype(BF16),
            qtb=(q * eb).astype(BF16),
            ktb=(kb * (1.0 / eb)).astype(BF16),
            qhf=(q * jnp.exp2(bf)).astype(BF16),
            qhb=(q * jnp.exp2(cb)).astype(BF16),
            khat=(kf * jnp.exp2(b_end - bf)).astype(BF16),
            decay=jnp.exp2(b_end),
        )

    def group_heads(x):
        for j in range(group):
            h = x["g"] * group + j
            yield j, slice(h * HEAD, (h + 1) * HEAD), slice(j * HEAD, (j + 1) * HEAD)

    def stage_scores(x):
        x["af"], x["ab"], x["oi"] = {}, {}, {}
        for j, hs, hl in group_heads(x):
            st_old = st_scr[:, hs]
            x["af"][j] = _dot_nt(x["qtf"][:, hl], x["ktf"][:, hl])
            x["ab"][j] = _dot_nt(x["qtb"][:, hl], x["ktb"][:, hl])
            qcat = jnp.concatenate([x["qhf"][:, hl], x["qhb"][:, hl]], axis=1)
            scat = jnp.concatenate([st_old.astype(BF16), sb_ref[x["c"], :, hs]], axis=1)
            x["oi"][j] = _dot_nt(qcat, scat)
            st_scr[:, hs] = st_old * x["decay"][:, hl] + _dot_tn(x["v16"][:, hl], x["khat"][:, hl])

    def stage_mix(x):
        x["o"] = {}
        for j, hs, hl in group_heads(x):
            a = (jnp.where(lower_m, x["af"][j], 0.0) + jnp.where(upper_m, x["ab"][j], 0.0)).astype(BF16)
            oi_scr[x["rows"], hs] = x["oi"][j]
            x["o"][j] = _dot(a, x["v16"][:, hl]) + x["oi"][j]

    def stage_output(x):
        for j, hs, hl in group_heads(x):
            y_scr[slot, x["rows"], hs] = _head_output(x["o"][j], x["gate"][:, hl], hn[:, hs], x["valid"],
                                                      y_scr.dtype)

    def chunk_redo(c):
        idx, _ = _chunk_in_seq(tile * cpt + c, lay)
        rows = pl.ds(pl.multiple_of(c * CHUNK, CHUNK), CHUNK)
        valid = _pad_row_mask(idx == 0)
        q, v16, kf, bf, kb, cb = _chunk_gates(q_ref, ff_ref, fb_ref, ip_ref, rows, slice(None),
                                              lb_f, lb_b, lower, upper)
        q_scr[...] = q
        kf_scr[...] = kf
        bf_scr[...] = bf
        kb_scr[...] = kb
        cb_scr[...] = cb
        gate = _silu(gt_ref[rows, :])
        for h in range(heads):
            hs = slice(h * HEAD, (h + 1) * HEAD)
            a = (_exact_scores(q_scr, kf_scr, bf_scr, hs, lower_m)
                 + _exact_scores(q_scr, kb_scr, cb_scr, hs, upper_m)).astype(BF16)
            o = _dot(a, v16[:, hs]) + oi_scr[rows, hs]
            y_scr[slot, rows, hs] = _head_output(o, gate[:, hs], hn[:, hs], valid, y_scr.dtype)

    def project_piece(n):
        cols = slice(n * OUT_PIECE, (n + 1) * OUT_PIECE)
        h1_ref[:, cols] = (h_ref[:, cols] + _dot(y_scr[1 - slot], w_ref[0:hgw, cols])
                           + _dot(yp_ref[...], w_ref[hgw:, cols]))

    pieces = h1_ref.shape[1] // OUT_PIECE
    every = max(1, len(slabs) // pieces)
    stages = (stage_operands, stage_scores, stage_mix, stage_output)
    in_flight = []
    issued = 0
    for k in range(len(slabs) + len(stages)):
        in_flight.insert(0, stage_gates(*slabs[k]) if k < len(slabs) else None)
        in_flight = in_flight[:len(stages) + 1]
        for depth, stage in enumerate(stages, start=1):
            if depth < len(in_flight) and in_flight[depth] is not None:
                stage(in_flight[depth])
        if k % every == every - 1 and issued < pieces:
            project_piece(issued)
            issued += 1
    for n in range(issued, pieces):
        project_piece(n)

    def redo_body(c, carry):
        @pl.when(redo_ref[c] != 0)
        def _():
            chunk_redo(c)
        return carry

    lax.fori_loop(0, cpt, redo_body, 0)


def _hgrn2_out(u, sb, lb, head_norm, h, y_pool, w_out, layer, *, lay, cpt, hgw):
    nc = _num_chunks(lay)
    nt = nc // cpt
    tr = cpt * CHUNK
    heads = hgw // HEAD
    d = h.shape[1]
    pw = y_pool.shape[1]
    kern = functools.partial(_hgrn2_kernel, lay=lay, cpt=cpt, heads=heads)
    scan_tile = lambda i: jnp.minimum(i, nt - 1)
    proj_tile = lambda i: jnp.maximum(i - 1, 0)
    col = lambda part: pl.BlockSpec((tr, hgw), lambda i: (scan_tile(i), part))
    chunk_f32 = pltpu.VMEM((CHUNK, hgw), F32)
    return pl.pallas_call(
        kern,
        grid=(nt + 1,),
        in_specs=[
            col(0), col(1), col(2), col(3), col(4),
            pl.BlockSpec((cpt, HEAD, hgw), lambda i: (scan_tile(i), 0, 0)),
            pl.BlockSpec((2, hgw), lambda i: (0, 0)),
            pl.BlockSpec((1, hgw), lambda i: (0, 0)),
            pl.BlockSpec((tr, d), lambda i: (proj_tile(i), 0)),
            pl.BlockSpec((tr, pw), lambda i: (proj_tile(i), 0)),
            pl.BlockSpec((None,) + w_out.shape[1:], lambda i: (layer, 0, 0), pipeline_mode=pl.Buffered(1)),
        ],
        out_specs=pl.BlockSpec((tr, d), lambda i: (proj_tile(i), 0)),
        out_shape=jax.ShapeDtypeStruct(h.shape, F32),
        scratch_shapes=[
            pltpu.VMEM((HEAD, hgw), F32),
            pltpu.VMEM((tr, hgw), F32),
            pltpu.SMEM((cpt,), jnp.int32),
            pltpu.VMEM((2, tr, hgw), BF16),
            chunk_f32, chunk_f32, chunk_f32, chunk_f32, chunk_f32,
        ],
        compiler_params=_params(("arbitrary",)),
        name="hgrn2_scan_out_proj",
    )(u, u, u, u, u, sb, lb, head_norm, h, y_pool, w_out)


def _pool_bands(tm):
    r = np.arange(tm)[:, None]
    j = np.arange(tm + 2 * POOL_HALO)[None, :] - POOL_HALO
    bands = [(j >= r - w // 2) & (j < r + w - w // 2) for w in POOL_WINDOWS]
    return jnp.asarray(np.stack(bands), BF16)


def _pool_kernel(prev_ref, x_ref, next_ref, band_ref, wp_ref, ps_ref, o_ref, *, lay, tm, sub):
    gw = x_ref.shape[1] // len(POOL_WINDOWS)
    groups = range(len(POOL_WINDOWS))
    cols = [slice(g * gw, (g + 1) * gw) for g in groups]

    def masked_rows(t):
        r0 = t * tm
        idx, cps = _chunk_in_seq((pl.program_id(0) * sub + t) * (tm // CHUNK), lay)
        p0 = idx * CHUNK
        seq_rows = cps * CHUNK
        pos_h = p0 - POOL_HALO + lax.broadcasted_iota(jnp.int32, (tm + 2 * POOL_HALO, 1), 0)
        valid_h = jnp.logical_and(pos_h >= META_PAD, pos_h < seq_rows)
        before = prev_ref[...] if t == 0 else x_ref[r0 - POOL_HALO:r0, :]
        after = next_ref[...] if t == sub - 1 else x_ref[r0 + tm:r0 + tm + POOL_HALO, :]
        x_all = jnp.concatenate([before, x_ref[r0:r0 + tm, :], after], axis=0)
        xm = jnp.where(valid_h, x_all, 0.0)
        hi = xm.astype(BF16)
        lo = (xm - hi.astype(F32)).astype(BF16)
        return dict(r0=r0, pos=p0 + lax.broadcasted_iota(jnp.int32, (tm, 1), 0), seq_rows=seq_rows,
                    xm=xm, hi=hi, lo=lo)

    tiles = [masked_rows(t) for t in range(sub)]
    for x in tiles:
        x["total"] = [_dot(band_ref[g], x["hi"][:, cols[g]]) + _dot(band_ref[g], x["lo"][:, cols[g]])
                      for g in groups]
    for x in tiles:
        x["pooled"] = []
        for g, window in enumerate(POOL_WINDOWS):
            back = window // 2
            count = (jnp.minimum(x["pos"] + (window - back), x["seq_rows"])
                     - jnp.maximum(x["pos"] - back, META_PAD))
            count = jnp.maximum(count, 1).astype(F32)
            centre = x["xm"][POOL_HALO:POOL_HALO + tm, cols[g]]
            x["pooled"].append((x["total"][g] / count - centre).astype(BF16))
    for x in tiles:
        x["y"] = [_dot(x["pooled"][g], wp_ref[g]) for g in groups]
    for x in tiles:
        valid = x["pos"] >= META_PAD
        for g in groups:
            y = jnp.where(valid, x["y"][g] * ps_ref[:, cols[g]], 0.0)
            o_ref[x["r0"]:x["r0"] + tm, cols[g]] = y.astype(o_ref.dtype)


def _pool(u, w_pool, layer, pool_scale, *, lay, tm, sub, hgw):
    rows = u.shape[0]
    pw = pool_scale.shape[1]
    col = 5 * hgw // pw
    bt = tm * sub
    hb = bt // POOL_HALO
    last_hb = rows // POOL_HALO - 1
    bands = _pool_bands(tm)
    kern = functools.partial(_pool_kernel, lay=lay, tm=tm, sub=sub)
    return pl.pallas_call(
        kern,
        grid=(rows // bt,),
        in_specs=[
            pl.BlockSpec((POOL_HALO, pw), lambda i: (jnp.maximum(i * hb - 1, 0), col)),
            pl.BlockSpec((bt, pw), lambda i: (i, col)),
            pl.BlockSpec((POOL_HALO, pw), lambda i: (jnp.minimum((i + 1) * hb, last_hb), col)),
            pl.BlockSpec(bands.shape, lambda i: (0, 0, 0)),
            pl.BlockSpec((None,) + w_pool.shape[1:], lambda i: (layer, 0, 0, 0)),
            pl.BlockSpec((1, pw), lambda i: (0, 0)),
        ],
        out_specs=pl.BlockSpec((bt, pw), lambda i: (i, 0)),
        out_shape=jax.ShapeDtypeStruct((rows, pw), BF16),
        compiler_params=_params(("parallel",)),
        name="pool_mixer",
    )(u, u, u, bands, w_pool, pool_scale)


def _mlp_kernel(h_ref, g_ref, wu_ref, wd_ref, *rest, final, f_axis):
    o_ref, m_scr = rest[-2:]
    f = pl.program_id(f_axis)

    def contribution(m):
        hidden = jnp.square(jnp.maximum(_dot(m, wu_ref[...]), 0.0)).astype(BF16)
        return _dot(hidden, wd_ref[...])

    @pl.when(f == 0)
    def _():
        h = h_ref[...]
        m = _rms(h, g_ref[...]).astype(BF16)
        m_scr[...] = m
        o_ref[...] = h + contribution(m)

    @pl.when(f > 0)
    def _():
        o_ref[...] += contribution(m_scr[...])

    if final:
        @pl.when(f == pl.num_programs(f_axis) - 1)
        def _():
            o_ref[...] = _rms(o_ref[...], rest[0][...])


def _mlp(h, gain, w_up, w_down, layer, *, bm, bf):
    rows, d = h.shape
    dff = w_up.shape[2]
    return pl.pallas_call(
        functools.partial(_mlp_kernel, final=False, f_axis=1),
        grid=(rows // bm, dff // bf),
        in_specs=[
            pl.BlockSpec((bm, d), lambda i, f: (i, 0)),
            pl.BlockSpec((1, d), lambda i, f: (0, 0)),
            pl.BlockSpec((None, d, bf), lambda i, f: (layer, 0, f)),
            pl.BlockSpec((None, bf, d), lambda i, f: (layer, f, 0)),
        ],
        out_specs=pl.BlockSpec((bm, d), lambda i, f: (i, 0)),
        out_shape=jax.ShapeDtypeStruct((rows, d), F32),
        scratch_shapes=[pltpu.VMEM((bm, d), BF16)],
        compiler_params=_params(("parallel", "arbitrary")),
        name="mlp",
    )(h, gain, w_up, w_down)


def _mlp_final(h, gain, w_up, w_down, layer, final_gain, *, row0, n_seq, seq_rows, s, bm, bf):
    d = h.shape[1]
    dff = w_up.shape[2]
    skip = seq_rows - s
    return pl.pallas_call(
        functools.partial(_mlp_kernel, final=True, f_axis=2),
        grid=(n_seq, s // bm, dff // bf),
        in_specs=[
            pl.BlockSpec((pl.Element(bm), pl.Element(d)),
                         lambda b, i, f: (pl.multiple_of(row0 + b * seq_rows + skip + i * bm, CHUNK), 0)),
            pl.BlockSpec((1, d), lambda b, i, f: (0, 0)),
            pl.BlockSpec((None, d, bf), lambda b, i, f: (layer, 0, f)),
            pl.BlockSpec((None, bf, d), lambda b, i, f: (layer, f, 0)),
            pl.BlockSpec((1, d), lambda b, i, f: (0, 0)),
        ],
        out_specs=pl.BlockSpec((None, bm, d), lambda b, i, f: (b, i, 0)),
        out_shape=jax.ShapeDtypeStruct((n_seq, s, d), F32),
        scratch_shapes=[pltpu.VMEM((bm, d), BF16)],
        compiler_params=_params(("parallel", "parallel", "arbitrary")),
        name="mlp_final",
    )(h, gain, w_up, w_down, final_gain)


def _pack_rows(xs, meta):
    d = meta.shape[1]
    lead = jnp.concatenate([jnp.zeros((META_PAD, d), F32), meta.astype(F32)], axis=0)
    pieces = []
    for x in xs:
        for b in range(x.shape[0]):
            pieces += [lead, x[b]]
    return jnp.concatenate(pieces, axis=0)


def _tile_plan(lay):
    nc = _num_chunks(lay)
    g = 1
    for d in range(1, min(lay.cpa, lay.cpb) + 1):
        if lay.cpa % d == 0 and lay.cpb % d == 0:
            g = d
    return dict(
        proj_bm=CHUNK * _largest_divisor(nc, 20),
        bwd_cpt=_largest_divisor(nc, 12),
        scan_cpt=_largest_divisor(nc, 6),
        pool_tm=CHUNK * _largest_divisor(g, 4),
        pool_sub=_largest_divisor(nc // _largest_divisor(g, 4), 4),
        mlp_bm=CHUNK * _largest_divisor(nc, 10),
    )


def kernel(x_prompt, x_sample, meta_tokens, w_in, w_pool, pool_scale, hg_lower_bound, hg_head_norm,
           w_out, norm_mix, norm_mlp, w_up, w_down, final_norm):
    depth, d, in_cols = w_in.shape
    pw = pool_scale.shape[1]
    hgw = hg_head_norm.shape[1]
    dff = w_up.shape[2]
    lead = N_META + META_PAD
    s_a, s_b = x_prompt.shape[1], x_sample.shape[1]
    assert in_cols == 5 * hgw + pw and hgw % HEAD == 0 and pw == hgw
    assert s_a % CHUNK == 0 and s_b % CHUNK == 0
    lay = Layout(x_prompt.shape[0], (s_a + lead) // CHUNK, x_sample.shape[0], (s_b + lead) // CHUNK)
    plan = _tile_plan(lay)
    proj_bn = in_cols // _largest_divisor(in_cols // 256, 6)
    mlp_bf = dff // _largest_divisor(dff // 256, 8)

    probs = jax.nn.softmax(hg_lower_bound.astype(F32), axis=1)
    lower = jnp.cumsum(probs, axis=1) - probs[:, :1]

    h = _pack_rows((x_prompt, x_sample), meta_tokens)
    row = lambda a: a.astype(F32).reshape(1, -1)
    w_in, w_pool, w_out, w_up, w_down = (w.astype(BF16) for w in (w_in, w_pool, w_out, w_up, w_down))
    for l in range(depth):
        u = _norm_proj(h, row(norm_mix[l]), w_in, l, bm=plan["proj_bm"], bn=proj_bn)
        sb = _bwd_states(u, lower[1:2, l], lay=lay, cpt=plan["bwd_cpt"], hgw=hgw)
        y_pool = _pool(u, w_pool, l, row(pool_scale[l]), lay=lay, tm=plan["pool_tm"],
                       sub=plan["pool_sub"], hgw=hgw)
        h = _hgrn2_out(u, sb, lower[:, l], row(hg_head_norm[l]), h, y_pool, w_out, l,
                       lay=lay, cpt=plan["scan_cpt"], hgw=hgw)
        mlp_w = (row(norm_mlp[l]), w_up, w_down, l)
        if l < depth - 1:
            h = _mlp(h, *mlp_w, bm=plan["mlp_bm"], bf=mlp_bf)

    def final(row0, n_seq, s):
        bm = CHUNK * _largest_divisor(s // CHUNK, 8)
        return _mlp_final(h, *mlp_w, row(final_norm), row0=row0, n_seq=n_seq, seq_rows=s + lead,
                          s=s, bm=bm, bf=2 * mlp_bf)

    return (final(0, lay.n_a, s_a), final(lay.n_a * lay.cpa * CHUNK, lay.n_b, s_b))
```

```python
import collections
import functools

import jax
import jax.numpy as jnp
import numpy as np
from jax import lax
from jax.experimental import pallas as pl
from jax.experimental.pallas import tpu as pltpu

N_META = 16
CHUNK = 64
META_PAD = (-N_META) % CHUNK
HEAD = 128
SUBLANES = 8
POOL_WINDOWS = (2, 4, 8, 16)
POOL_HALO = 8
OUT_PIECE = 256
EPS = 1e-6
FORGET_FLOOR = 1e-30
SAFE_LOG2_DECAY = 86.0
CLEAR_LOG2_DECAY = -1e30

VMEM_LIMIT_BYTES = 56 * 1024 * 1024

F32 = jnp.float32
BF16 = jnp.bfloat16

Layout = collections.namedtuple("Layout", "n_a cpa n_b cpb")


def _num_chunks(lay):
    return lay.n_a * lay.cpa + lay.n_b * lay.cpb


def _chunk_in_seq(cg, lay):
    na = lay.n_a * lay.cpa
    in_a = cg < na
    idx = jnp.where(in_a, lax.rem(cg, lay.cpa), lax.rem(jnp.maximum(cg - na, 0), lay.cpb))
    cps = jnp.where(in_a, lay.cpa, lay.cpb)
    return idx, cps


def _largest_divisor(n, cap):
    best = 1
    for d in range(1, n + 1):
        if n % d == 0 and d <= cap:
            best = d
    return best


def _rms(x, gain):
    ms = jnp.mean(x * x, axis=-1, keepdims=True)
    return x * lax.rsqrt(ms + EPS) * gain


def _sigmoid(x):
    return 1.0 / (1.0 + jnp.exp(-x))


def _silu(x):
    hx = 0.5 * x
    return hx * jnp.tanh(hx) + hx


def _dot(a, b):
    return jnp.dot(a, b, preferred_element_type=F32)


def _dot_nt(a, b):
    return lax.dot_general(a, b, (((1,), (1,)), ((), ())), preferred_element_type=F32)


def _dot_tn(a, b):
    return lax.dot_general(a, b, (((0,), (0,)), ((), ())), preferred_element_type=F32)


def _params(sem):
    return pltpu.CompilerParams(dimension_semantics=sem, vmem_limit_bytes=VMEM_LIMIT_BYTES)


def _norm_proj_kernel(h_ref, g_ref, w_ref, o_ref, a_scr):
    @pl.when(pl.program_id(1) == 0)
    def _():
        a = _rms(h_ref[...], g_ref[...]).astype(BF16)
        a_scr[...] = a
        o_ref[...] = _dot(a, w_ref[...])

    @pl.when(pl.program_id(1) > 0)
    def _():
        o_ref[...] = _dot(a_scr[...], w_ref[...])


def _norm_proj(h, gain, w, layer, *, bm, bn):
    rows, d = h.shape
    n = w.shape[2]
    return pl.pallas_call(
        _norm_proj_kernel,
        grid=(rows // bm, n // bn),
        in_specs=[
            pl.BlockSpec((bm, d), lambda i, j: (i, 0)),
            pl.BlockSpec((1, d), lambda i, j: (0, 0)),
            pl.BlockSpec((None, d, bn), lambda i, j: (layer, 0, j)),
        ],
        out_specs=pl.BlockSpec((bm, bn), lambda i, j: (i, j)),
        out_shape=jax.ShapeDtypeStruct((rows, n), F32),
        scratch_shapes=[pltpu.VMEM((bm, d), BF16)],
        compiler_params=_params(("parallel", "arbitrary")),
        name="norm_proj",
    )(h, gain, w)


def _forget(f_pre, lb):
    span = 1.0 - lb
    w = span * _sigmoid(f_pre)
    log2_f = jnp.log2(jnp.maximum(lb + w, FORGET_FLOOR))
    return log2_f, span - w


def _tri(lower):
    r = lax.broadcasted_iota(jnp.int32, (CHUNK, CHUNK), 0)
    c = lax.broadcasted_iota(jnp.int32, (CHUNK, CHUNK), 1)
    return (r >= c) if lower else (r <= c)


def _cumsum_rows(tri_bf16, g):
    hi = g.astype(BF16)
    lo = (g - hi.astype(F32)).astype(BF16)
    return _dot(tri_bf16, hi) + _dot(tri_bf16, lo)


def _pad_row_mask(is_first):
    r = lax.broadcasted_iota(jnp.int32, (CHUNK, 1), 0)
    return jnp.logical_or(jnp.logical_not(is_first), r >= META_PAD)


def _bwd_state_kernel(fb_ref, ip_ref, lb_ref, sb_ref, st_scr, *, lay, cpt, heads):
    tile = pl.num_programs(0) - 1 - pl.program_id(0)
    upper = jnp.where(_tri(False), 1.0, 0.0).astype(BF16)
    lb = lb_ref[...]

    @pl.when(pl.program_id(0) == 0)
    def _():
        st_scr[...] = jnp.zeros_like(st_scr)

    group = 2 if heads % 2 == 0 else 1
    slabs = [(c, g) for c in reversed(range(cpt)) for g in range(heads // group)]

    def stage_gates(c, g):
        idx, _ = _chunk_in_seq(tile * cpt + c, lay)
        rows = slice(c * CHUNK, (c + 1) * CHUNK)
        cols = slice(g * group * HEAD, (g + 1) * group * HEAD)
        log2_f, k = _forget(fb_ref[rows, cols], lb[:, cols])
        return dict(c=c, g=g, is_first=idx == 0, k=k,
                    v16=ip_ref[rows, cols].astype(BF16),
                    cb=_cumsum_rows(upper, log2_f))

    def stage_operands(x):
        c_end = jnp.where(x["is_first"], CLEAR_LOG2_DECAY, x["cb"][0:1, :])
        x["khat"] = (x["k"] * jnp.exp2(c_end - x["cb"])).astype(BF16)
        x["decay"] = jnp.exp2(c_end)

    def stage_update(x):
        for j in range(group):
            h = x["g"] * group + j
            hs = slice(h * HEAD, (h + 1) * HEAD)
            hl = slice(j * HEAD, (j + 1) * HEAD)
            st_old = st_scr[:, hs]
            sb_ref[x["c"], :, hs] = st_old.astype(BF16)
            st_scr[:, hs] = st_old * x["decay"][:, hl] + _dot_tn(x["v16"][:, hl], x["khat"][:, hl])

    stages = (stage_operands, stage_update)
    in_flight = []
    for step in range(len(slabs) + len(stages)):
        in_flight.insert(0, stage_gates(*slabs[step]) if step < len(slabs) else None)
        in_flight = in_flight[:len(stages) + 1]
        for depth, stage in enumerate(stages, start=1):
            if depth < len(in_flight) and in_flight[depth] is not None:
                stage(in_flight[depth])


def _bwd_states(u, lb_b, *, lay, cpt, hgw):
    nc = _num_chunks(lay)
    nt = nc // cpt
    tr = cpt * CHUNK
    heads = hgw // HEAD
    kern = functools.partial(_bwd_state_kernel, lay=lay, cpt=cpt, heads=heads)
    return pl.pallas_call(
        kern,
        grid=(nt,),
        in_specs=[
            pl.BlockSpec((tr, hgw), lambda j: (nt - 1 - j, 2)),
            pl.BlockSpec((tr, hgw), lambda j: (nt - 1 - j, 3)),
            pl.BlockSpec((1, hgw), lambda j: (0, 0)),
        ],
        out_specs=pl.BlockSpec((cpt, HEAD, hgw), lambda j: (nt - 1 - j, 0, 0)),
        out_shape=jax.ShapeDtypeStruct((nc, HEAD, hgw), BF16),
        scratch_shapes=[pltpu.VMEM((HEAD, hgw), F32)],
        compiler_params=_params(("arbitrary",)),
        name="hgrn2_bwd_states",
    )(u, u, lb_b)


def _chunk_gates(q_ref, ff_ref, fb_ref, ip_ref, rows, cols, lb_f, lb_b, lower, upper):
    q = _silu(q_ref[rows, cols])
    v16 = ip_ref[rows, cols].astype(BF16)
    gf, kf = _forget(ff_ref[rows, cols], lb_f[:, cols])
    gb, kb = _forget(fb_ref[rows, cols], lb_b[:, cols])
    bf = _cumsum_rows(lower, gf)
    cb = _cumsum_rows(upper, gb)
    return q, v16, kf, bf, kb, cb


def _head_output(o, gate, head_norm, valid, dtype):
    ms = jnp.mean(o * o, axis=-1, keepdims=True)
    y = o * lax.rsqrt(ms + EPS) * head_norm * gate
    return jnp.where(valid, y, 0.0).astype(dtype)


def _exact_scores(q_scr, k_scr, b_scr, hs, mask):
    qh = q_scr[:, hs]
    bh = b_scr[:, hs]
    lane = lax.broadcasted_iota(jnp.int32, (CHUNK, CHUNK), 1)

    def body(sg, acc):
        group = pl.ds(pl.multiple_of(sg * SUBLANES, SUBLANES), SUBLANES)
        k8 = k_scr[group, hs]
        b8 = b_scr[group, hs]
        for j in range(SUBLANES):
            e = jnp.exp2(jnp.minimum(bh - b8[j:j + 1, :], 0.0))
            col = jnp.sum(qh * k8[j:j + 1, :] * e, axis=1, keepdims=True)
            acc = jnp.where(lane == sg * SUBLANES + j, col, acc)
        return acc

    acc = lax.fori_loop(0, CHUNK // SUBLANES, body, jnp.zeros((CHUNK, CHUNK), F32))
    return jnp.where(mask, acc, 0.0)


def _hgrn2_kernel(q_ref, ff_ref, fb_ref, ip_ref, gt_ref, sb_ref, lb_ref, hn_ref, h_ref, yp_ref, w_ref,
                  h1_ref, st_scr, oi_scr, redo_ref, y_scr, q_scr, kf_scr, bf_scr, kb_scr, cb_scr,
                  *, lay, cpt, heads):
    step = pl.program_id(0)
    tile = jnp.minimum(step, pl.num_programs(0) - 2)
    slot = lax.rem(step, 2)
    hgw = heads * HEAD
    lower_m = _tri(True)
    upper_m = _tri(False)
    lower = jnp.where(lower_m, 1.0, 0.0).astype(BF16)
    upper = jnp.where(upper_m, 1.0, 0.0).astype(BF16)
    lb_f = lb_ref[0:1, :]
    lb_b = lb_ref[1:2, :]
    hn = hn_ref[...]
    half = CHUNK // 2

    @pl.when(step == 0)
    def _():
        st_scr[...] = jnp.zeros_like(st_scr)
        y_scr[1] = jnp.zeros(y_scr.shape[1:], y_scr.dtype)

    group = 2 if heads % 2 == 0 else 1
    slabs = [(c, g) for c in range(cpt) for g in range(heads // group)]
    margin = {}

    def stage_gates(c, g):
        idx, cps = _chunk_in_seq(tile * cpt + c, lay)
        rows = slice(c * CHUNK, (c + 1) * CHUNK)
        cols = slice(g * group * HEAD, (g + 1) * group * HEAD)
        q, v16, kf, bf, kb, cb = _chunk_gates(q_ref, ff_ref, fb_ref, ip_ref, rows, cols,
                                              lb_f, lb_b, lower, upper)
        return dict(c=c, g=g, rows=rows, valid=_pad_row_mask(idx == 0), is_last=idx == cps - 1,
                    q=q, v16=v16, kf=kf, bf=bf, kb=kb, cb=cb, gate=_silu(gt_ref[rows, cols]))

    def stage_operands(x):
        q, kf, bf, kb, cb = x["q"], x["kf"], x["bf"], x["kb"], x["cb"]
        b_last = bf[CHUNK - 1:CHUNK, :]
        c_first = cb[0:1, :]
        rf = bf[half - 1:half, :]
        rb = cb[half:half + 1, :]
        ef = jnp.exp2(bf - rf)
        eb = jnp.exp2(cb - rb)
        b_end = jnp.where(x["is_last"], CLEAR_LOG2_DECAY, b_last)
        q_big = jnp.log2(jnp.maximum(jnp.max(jnp.abs(q), axis=0, keepdims=True), 1.0))
        m = jnp.minimum(jnp.minimum(rf - q_big, b_last - rf), jnp.minimum(rb - q_big, c_first - rb))
        margin[x["c"]] = m if x["g"] == 0 else jnp.minimum(margin[x["c"]], m)
        if x["g"] == heads // group - 1:
            redo_ref[x["c"]] = jnp.where(jnp.min(margin[x["c"]]) < -SAFE_LOG2_DECAY, 1, 0)
        x.update(
            qtf=(q * ef).astype(BF16),
            ktf=(kf * (1.0 / ef)).astype(BF16),
            qtb=(q * eb).astype(BF16),
            ktb=(kb * (1.0 / eb)).astype(BF16),
            qhf=(q * jnp.exp2(bf)).astype(BF16),
            qhb=(q * jnp.exp2(cb)).astype(BF16),
            khat=(kf * jnp.exp2(b_end - bf)).astype(BF16),
            decay=jnp.exp2(b_end),
        )

    def group_heads(x):
        for j in range(group):
            h = x["g"] * group + j
            yield j, slice(h * HEAD, (h + 1) * HEAD), slice(j * HEAD, (j + 1) * HEAD)

    def stage_scores(x):
        x["af"], x["ab"], x["oi"] = {}, {}, {}
        for j, hs, hl in group_heads(x):
            st_old = st_scr[:, hs]
            x["af"][j] = _dot_nt(x["qtf"][:, hl], x["ktf"][:, hl])
            x["ab"][j] = _dot_nt(x["qtb"][:, hl], x["ktb"][:, hl])
            qcat = jnp.concatenate([x["qhf"][:, hl], x["qhb"][:, hl]], axis=1)
            scat = jnp.concatenate([st_old.astype(BF16), sb_ref[x["c"], :, hs]], axis=1)
            x["oi"][j] = _dot_nt(qcat, scat)
            st_scr[:, hs] = st_old * x["decay"][:, hl] + _dot_tn(x["v16"][:, hl], x["khat"][:, hl])

    def stage_mix(x):
        x["o"] = {}
        for j, hs, hl in group_heads(x):
            a = (jnp.where(lower_m, x["af"][j], 0.0) + jnp.where(upper_m, x["ab"][j], 0.0)).astype(BF16)
            oi_scr[x["rows"], hs] = x["oi"][j]
            x["o"][j] = _dot(a, x["v16"][:, hl]) + x["oi"][j]

    def stage_output(x):
        for j, hs, hl in group_heads(x):
            y_scr[slot, x["rows"], hs] = _head_output(x["o"][j], x["gate"][:, hl], hn[:, hs], x["valid"],
                                                      y_scr.dtype)

    def chunk_redo(c):
        idx, _ = _chunk_in_seq(tile * cpt + c, lay)
        rows = pl.ds(pl.multiple_of(c * CHUNK, CHUNK), CHUNK)
        valid = _pad_row_mask(idx == 0)
        q, v16, kf, bf, kb, cb = _chunk_gates(q_ref, ff_ref, fb_ref, ip_ref, rows, slice(None),
                                              lb_f, lb_b, lower, upper)
        q_scr[...] = q
        kf_scr[...] = kf
        bf_scr[...] = bf
        kb_scr[...] = kb
        cb_scr[...] = cb
        gate = _silu(gt_ref[rows, :])
        for h in range(heads):
            hs = slice(h * HEAD, (h + 1) * HEAD)
            a = (_exact_scores(q_scr, kf_scr, bf_scr, hs, lower_m)
                 + _exact_scores(q_scr, kb_scr, cb_scr, hs, upper_m)).astype(BF16)
            o = _dot(a, v16[:, hs]) + oi_scr[rows, hs]
            y_scr[slot, rows, hs] = _head_output(o, gate[:, hs], hn[:, hs], valid, y_scr.dtype)

    def project_piece(n):
        cols = slice(n * OUT_PIECE, (n + 1) * OUT_PIECE)
        h1_ref[:, cols] = (h_ref[:, cols] + _dot(y_scr[1 - slot], w_ref[0:hgw, cols])
                           + _dot(yp_ref[...], w_ref[hgw:, cols]))

    pieces = h1_ref.shape[1] // OUT_PIECE
    every = max(1, len(slabs) // pieces)
    stages = (stage_operands, stage_scores, stage_mix, stage_output)
    in_flight = []
    issued = 0
    for k in range(len(slabs) + len(stages)):
        in_flight.insert(0, stage_gates(*slabs[k]) if k < len(slabs) else None)
        in_flight = in_flight[:len(stages) + 1]
        for depth, stage in enumerate(stages, start=1):
            if depth < len(in_flight) and in_flight[depth] is not None:
                stage(in_flight[depth])
        if k % every == 0 and issued < pieces:
            project_piece(issued)
            issued += 1
    for n in range(issued, pieces):
        project_piece(n)

    def redo_body(c, carry):
        @pl.when(redo_ref[c] != 0)
        def _():
            chunk_redo(c)
        return carry

    lax.fori_loop(0, cpt, redo_body, 0)


def _hgrn2_out(u, sb, lb, head_norm, h, y_pool, w_out, layer, *, lay, cpt, hgw):
    nc = _num_chunks(lay)
    nt = nc // cpt
    tr = cpt * CHUNK
    heads = hgw // HEAD
    d = h.shape[1]
    pw = y_pool.shape[1]
    kern = functools.partial(_hgrn2_kernel, lay=lay, cpt=cpt, heads=heads)
    scan_tile = lambda i: jnp.minimum(i, nt - 1)
    proj_tile = lambda i: jnp.maximum(i - 1, 0)
    col = lambda part: pl.BlockSpec((tr, hgw), lambda i: (scan_tile(i), part))
    chunk_f32 = pltpu.VMEM((CHUNK, hgw), F32)
    return pl.pallas_call(
        kern,
        grid=(nt + 1,),
        in_specs=[
            col(0), col(1), col(2), col(3), col(4),
            pl.BlockSpec((cpt, HEAD, hgw), lambda i: (scan_tile(i), 0, 0)),
            pl.BlockSpec((2, hgw), lambda i: (0, 0)),
            pl.BlockSpec((1, hgw), lambda i: (0, 0)),
            pl.BlockSpec((tr, d), lambda i: (proj_tile(i), 0)),
            pl.BlockSpec((tr, pw), lambda i: (proj_tile(i), 0)),
            pl.BlockSpec((None,) + w_out.shape[1:], lambda i: (layer, 0, 0), pipeline_mode=pl.Buffered(1)),
        ],
        out_specs=pl.BlockSpec((tr, d), lambda i: (proj_tile(i), 0)),
        out_shape=jax.ShapeDtypeStruct(h.shape, F32),
        scratch_shapes=[
            pltpu.VMEM((HEAD, hgw), F32),
            pltpu.VMEM((tr, hgw), F32),
            pltpu.SMEM((cpt,), jnp.int32),
            pltpu.VMEM((2, tr, hgw), BF16),
            chunk_f32, chunk_f32, chunk_f32, chunk_f32, chunk_f32,
        ],
        compiler_params=_params(("arbitrary",)),
        name="hgrn2_scan_out_proj",
    )(u, u, u, u, u, sb, lb, head_norm, h, y_pool, w_out)


def _pool_bands(tm):
    r = np.arange(tm)[:, None]
    j = np.arange(tm + 2 * POOL_HALO)[None, :] - POOL_HALO
    bands = [(j >= r - w // 2) & (j < r + w - w // 2) for w in POOL_WINDOWS]
    return jnp.asarray(np.stack(bands), BF16)


def _pool_kernel(prev_ref, x_ref, next_ref, band_ref, wp_ref, ps_ref, o_ref, *, lay, tm, sub):
    gw = x_ref.shape[1] // len(POOL_WINDOWS)
    groups = range(len(POOL_WINDOWS))
    cols = [slice(g * gw, (g + 1) * gw) for g in groups]

    def masked_rows(t):
        r0 = t * tm
        idx, cps = _chunk_in_seq((pl.program_id(0) * sub + t) * (tm // CHUNK), lay)
        p0 = idx * CHUNK
        seq_rows = cps * CHUNK
        pos_h = p0 - POOL_HALO + lax.broadcasted_iota(jnp.int32, (tm + 2 * POOL_HALO, 1), 0)
        valid_h = jnp.logical_and(pos_h >= META_PAD, pos_h < seq_rows)
        before = prev_ref[...] if t == 0 else x_ref[r0 - POOL_HALO:r0, :]
        after = next_ref[...] if t == sub - 1 else x_ref[r0 + tm:r0 + tm + POOL_HALO, :]
        x_all = jnp.concatenate([before, x_ref[r0:r0 + tm, :], after], axis=0)
        xm = jnp.where(valid_h, x_all, 0.0)
        hi = xm.astype(BF16)
        lo = (xm - hi.astype(F32)).astype(BF16)
        return dict(r0=r0, pos=p0 + lax.broadcasted_iota(jnp.int32, (tm, 1), 0), seq_rows=seq_rows,
                    xm=xm, hi=hi, lo=lo)

    tiles = [masked_rows(t) for t in range(sub)]
    for x in tiles:
        x["total"] = [_dot(band_ref[g], x["hi"][:, cols[g]]) + _dot(band_ref[g], x["lo"][:, cols[g]])
                      for g in groups]
    for x in tiles:
        x["pooled"] = []
        for g, window in enumerate(POOL_WINDOWS):
            back = window // 2
            count = (jnp.minimum(x["pos"] + (window - back), x["seq_rows"])
                     - jnp.maximum(x["pos"] - back, META_PAD))
            count = jnp.maximum(count, 1).astype(F32)
            centre = x["xm"][POOL_HALO:POOL_HALO + tm, cols[g]]
            x["pooled"].append((x["total"][g] / count - centre).astype(BF16))
    for x in tiles:
        x["y"] = [_dot(x["pooled"][g], wp_ref[g]) for g in groups]
    for x in tiles:
        valid = x["pos"] >= META_PAD
        for g in groups:
            y = jnp.where(valid, x["y"][g] * ps_ref[:, cols[g]], 0.0)
            o_ref[x["r0"]:x["r0"] + tm, cols[g]] = y.astype(o_ref.dtype)


def _pool(u, w_pool, layer, pool_scale, *, lay, tm, sub, hgw):
    rows = u.shape[0]
    pw = pool_scale.shape[1]
    col = 5 * hgw // pw
    bt = tm * sub
    hb = bt // POOL_HALO
    last_hb = rows // POOL_HALO - 1
    bands = _pool_bands(tm)
    kern = functools.partial(_pool_kernel, lay=lay, tm=tm, sub=sub)
    return pl.pallas_call(
        kern,
        grid=(rows // bt,),
        in_specs=[
            pl.BlockSpec((POOL_HALO, pw), lambda i: (jnp.maximum(i * hb - 1, 0), col)),
            pl.BlockSpec((bt, pw), lambda i: (i, col)),
            pl.BlockSpec((POOL_HALO, pw), lambda i: (jnp.minimum((i + 1) * hb, last_hb), col)),
            pl.BlockSpec(bands.shape, lambda i: (0, 0, 0)),
            pl.BlockSpec((None,) + w_pool.shape[1:], lambda i: (layer, 0, 0, 0)),
            pl.BlockSpec((1, pw), lambda i: (0, 0)),
        ],
        out_specs=pl.BlockSpec((bt, pw), lambda i: (i, 0)),
        out_shape=jax.ShapeDtypeStruct((rows, pw), BF16),
        compiler_params=_params(("parallel",)),
        name="pool_mixer",
    )(u, u, u, bands, w_pool, pool_scale)


def _mlp_kernel(h_ref, g_ref, wu_ref, wd_ref, *rest, final, f_axis):
    o_ref, m_scr = rest[-2:]
    f = pl.program_id(f_axis)

    def contribution(m):
        hidden = jnp.square(jnp.maximum(_dot(m, wu_ref[...]), 0.0)).astype(BF16)
        return _dot(hidden, wd_ref[...])

    @pl.when(f == 0)
    def _():
        h = h_ref[...]
        m = _rms(h, g_ref[...]).astype(BF16)
        m_scr[...] = m
        o_ref[...] = h + contribution(m)

    @pl.when(f > 0)
    def _():
        o_ref[...] += contribution(m_scr[...])

    if final:
        @pl.when(f == pl.num_programs(f_axis) - 1)
        def _():
            o_ref[...] = _rms(o_ref[...], rest[0][...])


def _mlp(h, gain, w_up, w_down, layer, *, bm, bf):
    rows, d = h.shape
    dff = w_up.shape[2]
    return pl.pallas_call(
        functools.partial(_mlp_kernel, final=False, f_axis=1),
        grid=(rows // bm, dff // bf),
        in_specs=[
            pl.BlockSpec((bm, d), lambda i, f: (i, 0)),
            pl.BlockSpec((1, d), lambda i, f: (0, 0)),
            pl.BlockSpec((None, d, bf), lambda i, f: (layer, 0, f)),
            pl.BlockSpec((None, bf, d), lambda i, f: (layer, f, 0)),
        ],
        out_specs=pl.BlockSpec((bm, d), lambda i, f: (i, 0)),
        out_shape=jax.ShapeDtypeStruct((rows, d), F32),
        scratch_shapes=[pltpu.VMEM((bm, d), BF16)],
        compiler_params=_params(("parallel", "arbitrary")),
        name="mlp",
    )(h, gain, w_up, w_down)


def _mlp_final(h, gain, w_up, w_down, layer, final_gain, *, row0, n_seq, seq_rows, s, bm, bf):
    d = h.shape[1]
    dff = w_up.shape[2]
    skip = seq_rows - s
    return pl.pallas_call(
        functools.partial(_mlp_kernel, final=True, f_axis=2),
        grid=(n_seq, s // bm, dff // bf),
        in_specs=[
            pl.BlockSpec((pl.Element(bm), pl.Element(d)),
                         lambda b, i, f: (pl.multiple_of(row0 + b * seq_rows + skip + i * bm, CHUNK), 0)),
            pl.BlockSpec((1, d), lambda b, i, f: (0, 0)),
            pl.BlockSpec((None, d, bf), lambda b, i, f: (layer, 0, f)),
            pl.BlockSpec((None, bf, d), lambda b, i, f: (layer, f, 0)),
            pl.BlockSpec((1, d), lambda b, i, f: (0, 0)),
        ],
        out_specs=pl.BlockSpec((None, bm, d), lambda b, i, f: (b, i, 0)),
        out_shape=jax.ShapeDtypeStruct((n_seq, s, d), F32),
        scratch_shapes=[pltpu.VMEM((bm, d), BF16)],
        compiler_params=_params(("parallel", "parallel", "arbitrary")),
        name="mlp_final",
    )(h, gain, w_up, w_down, final_gain)


def _pack_rows(xs, meta):
    d = meta.shape[1]
    lead = jnp.concatenate([jnp.zeros((META_PAD, d), F32), meta.astype(F32)], axis=0)
    pieces = []
    for x in xs:
        for b in range(x.shape[0]):
            pieces += [lead, x[b]]
    return jnp.concatenate(pieces, axis=0)


def _tile_plan(lay):
    nc = _num_chunks(lay)
    g = 1
    for d in range(1, min(lay.cpa, lay.cpb) + 1):
        if lay.cpa % d == 0 and lay.cpb % d == 0:
            g = d
    return dict(
        proj_bm=CHUNK * _largest_divisor(nc, 20),
        bwd_cpt=_largest_divisor(nc, 12),
        scan_cpt=_largest_divisor(nc, 6),
        pool_tm=CHUNK * _largest_divisor(g, 4),
        pool_sub=_largest_divisor(nc // _largest_divisor(g, 4), 4),
        mlp_bm=CHUNK * _largest_divisor(nc, 10),
    )


def kernel(x_prompt, x_sample, meta_tokens, w_in, w_pool, pool_scale, hg_lower_bound, hg_head_norm,
           w_out, norm_mix, norm_mlp, w_up, w_down, final_norm):
    depth, d, in_cols = w_in.shape
    pw = pool_scale.shape[1]
    hgw = hg_head_norm.shape[1]
    dff = w_up.shape[2]
    lead = N_META + META_PAD
    s_a, s_b = x_prompt.shape[1], x_sample.shape[1]
    assert in_cols == 5 * hgw + pw and hgw % HEAD == 0 and pw == hgw
    assert s_a % CHUNK == 0 and s_b % CHUNK == 0
    lay = Layout(x_prompt.shape[0], (s_a + lead) // CHUNK, x_sample.shape[0], (s_b + lead) // CHUNK)
    plan = _tile_plan(lay)
    proj_bn = in_cols // _largest_divisor(in_cols // 256, 6)
    mlp_bf = dff // _largest_divisor(dff // 256, 8)

    probs = jax.nn.softmax(hg_lower_bound.astype(F32), axis=1)
    lower = jnp.cumsum(probs, axis=1) - probs[:, :1]

    h = _pack_rows((x_prompt, x_sample), meta_tokens)
    row = lambda a: a.astype(F32).reshape(1, -1)
    w_in, w_pool, w_out, w_up, w_down = (w.astype(BF16) for w in (w_in, w_pool, w_out, w_up, w_down))
    for l in range(depth):
        u = _norm_proj(h, row(norm_mix[l]), w_in, l, bm=plan["proj_bm"], bn=proj_bn)
        sb = _bwd_states(u, lower[1:2, l], lay=lay, cpt=plan["bwd_cpt"], hgw=hgw)
        y_pool = _pool(u, w_pool, l, row(pool_scale[l]), lay=lay, tm=plan["pool_tm"],
                       sub=plan["pool_sub"], hgw=hgw)
        h = _hgrn2_out(u, sb, lower[:, l], row(hg_head_norm[l]), h, y_pool, w_out, l,
                       lay=lay, cpt=plan["scan_cpt"], hgw=hgw)
        mlp_w = (row(norm_mlp[l]), w_up, w_down, l)
        if l < depth - 1:
            h = _mlp(h, *mlp_w, bm=plan["mlp_bm"], bf=mlp_bf)

    def final(row0, n_seq, s):
        bm = CHUNK * _largest_divisor(s // CHUNK, 8)
        return _mlp_final(h, *mlp_w, row(final_norm), row0=row0, n_seq=n_seq, seq_rows=s + lead,
                          s=s, bm=bm, bf=2 * mlp_bf)

    return (final(0, lay.n_a, s_a), final(lay.n_a * lay.cpa * CHUNK, lay.n_b, s_b))
```

```python
import collections
import functools

import jax
import jax.numpy as jnp
import numpy as np
from jax import lax
from jax.experimental import pallas as pl
from jax.experimental.pallas import tpu as pltpu

N_META = 16
CHUNK = 64
META_PAD = (-N_META) % CHUNK
HEAD = 128
SUBLANES = 8
POOL_WINDOWS = (2, 4, 8, 16)
POOL_HALO = 8
OUT_PIECE = 256
EPS = 1e-6
FORGET_FLOOR = 1e-30
SAFE_LOG2_DECAY = 86.0
CLEAR_LOG2_DECAY = -1e30

VMEM_LIMIT_BYTES = 56 * 1024 * 1024

F32 = jnp.float32
BF16 = jnp.bfloat16

Layout = collections.namedtuple("Layout", "n_a cpa n_b cpb")


def _num_chunks(lay):
    return lay.n_a * lay.cpa + lay.n_b * lay.cpb


def _chunk_in_seq(cg, lay):
    na = lay.n_a * lay.cpa
    in_a = cg < na
    idx = jnp.where(in_a, lax.rem(cg, lay.cpa), lax.rem(jnp.maximum(cg - na, 0), lay.cpb))
    cps = jnp.where(in_a, lay.cpa, lay.cpb)
    return idx, cps


def _largest_divisor(n, cap):
    best = 1
    for d in range(1, n + 1):
        if n % d == 0 and d <= cap:
            best = d
    return best


def _rms(x, gain):
    ms = jnp.mean(x * x, axis=-1, keepdims=True)
    return x * lax.rsqrt(ms + EPS) * gain


def _sigmoid(x):
    return 1.0 / (1.0 + jnp.exp(-x))


def _silu(x):
    hx = 0.5 * x
    return hx * jnp.tanh(hx) + hx


def _dot(a, b):
    return jnp.dot(a, b, preferred_element_type=F32)


def _dot_nt(a, b):
    return lax.dot_general(a, b, (((1,), (1,)), ((), ())), preferred_element_type=F32)


def _dot_tn(a, b):
    return lax.dot_general(a, b, (((0,), (0,)), ((), ())), preferred_element_type=F32)


def _params(sem):
    return pltpu.CompilerParams(dimension_semantics=sem, vmem_limit_bytes=VMEM_LIMIT_BYTES)


def _norm_proj_kernel(h_ref, g_ref, w_ref, o_ref, a_scr):
    @pl.when(pl.program_id(1) == 0)
    def _():
        a = _rms(h_ref[...], g_ref[...]).astype(BF16)
        a_scr[...] = a
        o_ref[...] = _dot(a, w_ref[...])

    @pl.when(pl.program_id(1) > 0)
    def _():
        o_ref[...] = _dot(a_scr[...], w_ref[...])


def _norm_proj(h, gain, w, layer, *, bm, bn):
    rows, d = h.shape
    n = w.shape[2]
    return pl.pallas_call(
        _norm_proj_kernel,
        grid=(rows // bm, n // bn),
        in_specs=[
            pl.BlockSpec((bm, d), lambda i, j: (i, 0)),
            pl.BlockSpec((1, d), lambda i, j: (0, 0)),
            pl.BlockSpec((None, d, bn), lambda i, j: (layer, 0, j)),
        ],
        out_specs=pl.BlockSpec((bm, bn), lambda i, j: (i, j)),
        out_shape=jax.ShapeDtypeStruct((rows, n), F32),
        scratch_shapes=[pltpu.VMEM((bm, d), BF16)],
        compiler_params=_params(("parallel", "arbitrary")),
        name="norm_proj",
    )(h, gain, w)


def _forget(f_pre, lb):
    span = 1.0 - lb
    w = span * _sigmoid(f_pre)
    log2_f = jnp.log2(jnp.maximum(lb + w, FORGET_FLOOR))
    return log2_f, span - w


def _tri(lower):
    r = lax.broadcasted_iota(jnp.int32, (CHUNK, CHUNK), 0)
    c = lax.broadcasted_iota(jnp.int32, (CHUNK, CHUNK), 1)
    return (r >= c) if lower else (r <= c)


def _cumsum_rows(tri_bf16, g):
    hi = g.astype(BF16)
    lo = (g - hi.astype(F32)).astype(BF16)
    return _dot(tri_bf16, hi) + _dot(tri_bf16, lo)


def _pad_row_mask(is_first):
    r = lax.broadcasted_iota(jnp.int32, (CHUNK, 1), 0)
    return jnp.logical_or(jnp.logical_not(is_first), r >= META_PAD)


def _bwd_state_kernel(fb_ref, ip_ref, lb_ref, sb_ref, st_scr, *, lay, cpt, heads):
    tile = pl.num_programs(0) - 1 - pl.program_id(0)
    upper = jnp.where(_tri(False), 1.0, 0.0).astype(BF16)
    lb = lb_ref[...]

    @pl.when(pl.program_id(0) == 0)
    def _():
        st_scr[...] = jnp.zeros_like(st_scr)

    group = 2 if heads % 2 == 0 else 1
    slabs = [(c, g) for c in reversed(range(cpt)) for g in range(heads // group)]

    def stage_gates(c, g):
        idx, _ = _chunk_in_seq(tile * cpt + c, lay)
        rows = slice(c * CHUNK, (c + 1) * CHUNK)
        cols = slice(g * group * HEAD, (g + 1) * group * HEAD)
        log2_f, k = _forget(fb_ref[rows, cols], lb[:, cols])
        return dict(c=c, g=g, is_first=idx == 0, k=k,
                    v16=ip_ref[rows, cols].astype(BF16),
                    cb=_cumsum_rows(upper, log2_f))

    def stage_operands(x):
        c_end = jnp.where(x["is_first"], CLEAR_LOG2_DECAY, x["cb"][0:1, :])
        x["khat"] = (x["k"] * jnp.exp2(c_end - x["cb"])).astype(BF16)
        x["decay"] = jnp.exp2(c_end)

    def stage_update(x):
        for j in range(group):
            h = x["g"] * group + j
            hs = slice(h * HEAD, (h + 1) * HEAD)
            hl = slice(j * HEAD, (j + 1) * HEAD)
            st_old = st_scr[:, hs]
            sb_ref[x["c"], :, hs] = st_old.astype(BF16)
            st_scr[:, hs] = st_old * x["decay"][:, hl] + _dot_tn(x["v16"][:, hl], x["khat"][:, hl])

    stages = (stage_operands, stage_update)
    in_flight = []
    for step in range(len(slabs) + len(stages)):
        in_flight.insert(0, stage_gates(*slabs[step]) if step < len(slabs) else None)
        in_flight = in_flight[:len(stages) + 1]
        for depth, stage in enumerate(stages, start=1):
            if depth < len(in_flight) and in_flight[depth] is not None:
                stage(in_flight[depth])


def _bwd_states(u, lb_b, *, lay, cpt, hgw):
    nc = _num_chunks(lay)
    nt = nc // cpt
    tr = cpt * CHUNK
    heads = hgw // HEAD
    kern = functools.partial(_bwd_state_kernel, lay=lay, cpt=cpt, heads=heads)
    return pl.pallas_call(
        kern,
        grid=(nt,),
        in_specs=[
            pl.BlockSpec((tr, hgw), lambda j: (nt - 1 - j, 2)),
            pl.BlockSpec((tr, hgw), lambda j: (nt - 1 - j, 3)),
            pl.BlockSpec((1, hgw), lambda j: (0, 0)),
        ],
        out_specs=pl.BlockSpec((cpt, HEAD, hgw), lambda j: (nt - 1 - j, 0, 0)),
        out_shape=jax.ShapeDtypeStruct((nc, HEAD, hgw), BF16),
        scratch_shapes=[pltpu.VMEM((HEAD, hgw), F32)],
        compiler_params=_params(("arbitrary",)),
        name="hgrn2_bwd_states",
    )(u, u, lb_b)


def _chunk_gates(q_ref, ff_ref, fb_ref, ip_ref, rows, cols, lb_f, lb_b, lower, upper):
    q = _silu(q_ref[rows, cols])
    v16 = ip_ref[rows, cols].astype(BF16)
    gf, kf = _forget(ff_ref[rows, cols], lb_f[:, cols])
    gb, kb = _forget(fb_ref[rows, cols], lb_b[:, cols])
    bf = _cumsum_rows(lower, gf)
    cb = _cumsum_rows(upper, gb)
    return q, v16, kf, bf, kb, cb


def _head_output(o, gate, head_norm, valid, dtype):
    ms = jnp.mean(o * o, axis=-1, keepdims=True)
    y = o * lax.rsqrt(ms + EPS) * head_norm * gate
    return jnp.where(valid, y, 0.0).astype(dtype)


def _exact_scores(q_scr, k_scr, b_scr, hs, mask):
    qh = q_scr[:, hs]
    bh = b_scr[:, hs]
    lane = lax.broadcasted_iota(jnp.int32, (CHUNK, CHUNK), 1)

    def body(sg, acc):
        group = pl.ds(pl.multiple_of(sg * SUBLANES, SUBLANES), SUBLANES)
        k8 = k_scr[group, hs]
        b8 = b_scr[group, hs]
        for j in range(SUBLANES):
            e = jnp.exp2(jnp.minimum(bh - b8[j:j + 1, :], 0.0))
            col = jnp.sum(qh * k8[j:j + 1, :] * e, axis=1, keepdims=True)
            acc = jnp.where(lane == sg * SUBLANES + j, col, acc)
        return acc

    acc = lax.fori_loop(0, CHUNK // SUBLANES, body, jnp.zeros((CHUNK, CHUNK), F32))
    return jnp.where(mask, acc, 0.0)


def _hgrn2_kernel(q_ref, ff_ref, fb_ref, ip_ref, gt_ref, sb_ref, lb_ref, hn_ref, h_ref, yp_ref, w_ref,
                  h1_ref, st_scr, oi_scr, redo_ref, y_scr, q_scr, kf_scr, bf_scr, kb_scr, cb_scr,
                  *, lay, cpt, heads):
    step = pl.program_id(0)
    tile = jnp.minimum(step, pl.num_programs(0) - 2)
    slot = lax.rem(step, 2)
    hgw = heads * HEAD
    lower_m = _tri(True)
    upper_m = _tri(False)
    lower = jnp.where(lower_m, 1.0, 0.0).astype(BF16)
    upper = jnp.where(upper_m, 1.0, 0.0).astype(BF16)
    lb_f = lb_ref[0:1, :]
    lb_b = lb_ref[1:2, :]
    hn = hn_ref[...]
    half = CHUNK // 2

    @pl.when(step == 0)
    def _():
        st_scr[...] = jnp.zeros_like(st_scr)
        y_scr[1] = jnp.zeros(y_scr.shape[1:], y_scr.dtype)

    group = 2 if heads % 2 == 0 else 1
    slabs = [(c, g) for c in range(cpt) for g in range(heads // group)]
    margin = {}

    def stage_gates(c, g):
        idx, cps = _chunk_in_seq(tile * cpt + c, lay)
        rows = slice(c * CHUNK, (c + 1) * CHUNK)
        cols = slice(g * group * HEAD, (g + 1) * group * HEAD)
        q, v16, kf, bf, kb, cb = _chunk_gates(q_ref, ff_ref, fb_ref, ip_ref, rows, cols,
                                              lb_f, lb_b, lower, upper)
        return dict(c=c, g=g, rows=rows, valid=_pad_row_mask(idx == 0), is_last=idx == cps - 1,
                    q=q, v16=v16, kf=kf, bf=bf, kb=kb, cb=cb, gate=_silu(gt_ref[rows, cols]))

    def stage_operands(x):
        q, kf, bf, kb, cb = x["q"], x["kf"], x["bf"], x["kb"], x["cb"]
        b_last = bf[CHUNK - 1:CHUNK, :]
        c_first = cb[0:1, :]
        rf = bf[half - 1:half, :]
        rb = cb[half:half + 1, :]
        ef = jnp.exp2(bf - rf)
        eb = jnp.exp2(cb - rb)
        b_end = jnp.where(x["is_last"], CLEAR_LOG2_DECAY, b_last)
        q_big = jnp.log2(jnp.maximum(jnp.max(jnp.abs(q), axis=0, keepdims=True), 1.0))
        m = jnp.minimum(jnp.minimum(rf - q_big, b_last - rf), jnp.minimum(rb - q_big, c_first - rb))
        margin[x["c"]] = m if x["g"] == 0 else jnp.minimum(margin[x["c"]], m)
        if x["g"] == heads // group - 1:
            redo_ref[x["c"]] = jnp.where(jnp.min(margin[x["c"]]) < -SAFE_LOG2_DECAY, 1, 0)
        x.update(
            qtf=(q * ef).astype(BF16),
            ktf=(kf * (1.0 / ef)).astype(BF16),
            qtb=(q * eb).astype(BF16),
            ktb=(kb * (1.0 / eb)).astype(BF16),
            qhf=(q * jnp.exp2(bf)).astype(BF16),
            qhb=(q * jnp.exp2(cb)).astype(BF16),
            khat=(kf * jnp.exp2(b_end - bf)).astype(BF16),
            decay=jnp.exp2(b_end),
        )

    def group_heads(x):
        for j in range(group):
            h = x["g"] * group + j
            yield j, slice(h * HEAD, (h + 1) * HEAD), slice(j * HEAD, (j + 1) * HEAD)

    def stage_scores(x):
        x["af"], x["ab"], x["oi"] = {}, {}, {}
        for j, hs, hl in group_heads(x):
            st_old = st_scr[:, hs]
            x["af"][j] = _dot_nt(x["qtf"][:, hl], x["ktf"][:, hl])
            x["ab"][j] = _dot_nt(x["qtb"][:, hl], x["ktb"][:, hl])
            qcat = jnp.concatenate([x["qhf"][:, hl], x["qhb"][:, hl]], axis=1)
            scat = jnp.concatenate([st_old.astype(BF16), sb_ref[x["c"], :, hs]], axis=1)
            x["oi"][j] = _dot_nt(qcat, scat)
            st_scr[:, hs] = st_old * x["decay"][:, hl] + _dot_tn(x["v16"][:, hl], x["khat"][:, hl])

    def stage_mix(x):
        x["o"] = {}
        for j, hs, hl in group_heads(x):
            a = (jnp.where(lower_m, x["af"][j], 0.0) + jnp.where(upper_m, x["ab"][j], 0.0)).astype(BF16)
            oi_scr[x["rows"], hs] = x["oi"][j]
            x["o"][j] = _dot(a, x["v16"][:, hl]) + x["oi"][j]

    def stage_output(x):
        for j, hs, hl in group_heads(x):
            y_scr[slot, x["rows"], hs] = _head_output(x["o"][j], x["gate"][:, hl], hn[:, hs], x["valid"],
                                                      y_scr.dtype)

    def chunk_redo(c):
        idx, _ = _chunk_in_seq(tile * cpt + c, lay)
        rows = pl.ds(pl.multiple_of(c * CHUNK, CHUNK), CHUNK)
        valid = _pad_row_mask(idx == 0)
        q, v16, kf, bf, kb, cb = _chunk_gates(q_ref, ff_ref, fb_ref, ip_ref, rows, slice(None),
                                              lb_f, lb_b, lower, upper)
        q_scr[...] = q
        kf_scr[...] = kf
        bf_scr[...] = bf
        kb_scr[...] = kb
        cb_scr[...] = cb
        gate = _silu(gt_ref[rows, :])
        for h in range(heads):
            hs = slice(h * HEAD, (h + 1) * HEAD)
            a = (_exact_scores(q_scr, kf_scr, bf_scr, hs, lower_m)
                 + _exact_scores(q_scr, kb_scr, cb_scr, hs, upper_m)).astype(BF16)
            o = _dot(a, v16[:, hs]) + oi_scr[rows, hs]
            y_scr[slot, rows, hs] = _head_output(o, gate[:, hs], hn[:, hs], valid, y_scr.dtype)

    def project_piece(n):
        cols = slice(n * OUT_PIECE, (n + 1) * OUT_PIECE)
        h1_ref[:, cols] = (h_ref[:, cols] + _dot(y_scr[1 - slot], w_ref[0:hgw, cols])
                           + _dot(yp_ref[...], w_ref[hgw:, cols]))

    pieces = h1_ref.shape[1] // OUT_PIECE
    every = max(1, len(slabs) // pieces)
    stages = (stage_operands, stage_scores, stage_mix, stage_output)
    in_flight = []
    issued = 0
    for k in range(len(slabs) + len(stages)):
        in_flight.insert(0, stage_gates(*slabs[k]) if k < len(slabs) else None)
        in_flight = in_flight[:len(stages) + 1]
        for depth, stage in enumerate(stages, start=1):
            if depth < len(in_flight) and in_flight[depth] is not None:
                stage(in_flight[depth])
        if k % every == 0 and issued < pieces:
            project_piece(issued)
            issued += 1
    for n in range(issued, pieces):
        project_piece(n)

    def redo_body(c, carry):
        @pl.when(redo_ref[c] != 0)
        def _():
            chunk_redo(c)
        return carry

    lax.fori_loop(0, cpt, redo_body, 0)


def _hgrn2_out(u, sb, lb, head_norm, h, y_pool, w_out, layer, *, lay, cpt, hgw):
    nc = _num_chunks(lay)
    nt = nc // cpt
    tr = cpt * CHUNK
    heads = hgw // HEAD
    d = h.shape[1]
    pw = y_pool.shape[1]
    kern = functools.partial(_hgrn2_kernel, lay=lay, cpt=cpt, heads=heads)
    scan_tile = lambda i: jnp.minimum(i, nt - 1)
    proj_tile = lambda i: jnp.maximum(i - 1, 0)
    col = lambda part: pl.BlockSpec((tr, hgw), lambda i: (scan_tile(i), part))
    chunk_f32 = pltpu.VMEM((CHUNK, hgw), F32)
    return pl.pallas_call(
        kern,
        grid=(nt + 1,),
        in_specs=[
            col(0), col(1), col(2), col(3), col(4),
            pl.BlockSpec((cpt, HEAD, hgw), lambda i: (scan_tile(i), 0, 0)),
            pl.BlockSpec((2, hgw), lambda i: (0, 0)),
            pl.BlockSpec((1, hgw), lambda i: (0, 0)),
            pl.BlockSpec((tr, d), lambda i: (proj_tile(i), 0)),
            pl.BlockSpec((tr, pw), lambda i: (proj_tile(i), 0)),
            pl.BlockSpec((None,) + w_out.shape[1:], lambda i: (layer, 0, 0), pipeline_mode=pl.Buffered(1)),
        ],
        out_specs=pl.BlockSpec((tr, d), lambda i: (proj_tile(i), 0)),
        out_shape=jax.ShapeDtypeStruct(h.shape, F32),
        scratch_shapes=[
            pltpu.VMEM((HEAD, hgw), F32),
            pltpu.VMEM((tr, hgw), F32),
            pltpu.SMEM((cpt,), jnp.int32),
            pltpu.VMEM((2, tr, hgw), BF16),
            chunk_f32, chunk_f32, chunk_f32, chunk_f32, chunk_f32,
        ],
        compiler_params=_params(("arbitrary",)),
        name="hgrn2_scan_out_proj",
    )(u, u, u, u, u, sb, lb, head_norm, h, y_pool, w_out)


def _pool_bands(tm):
    r = np.arange(tm)[:, None]
    j = np.arange(tm + 2 * POOL_HALO)[None, :] - POOL_HALO
    bands = [(j >= r - w // 2) & (j < r + w - w // 2) for w in POOL_WINDOWS]
    return jnp.asarray(np.stack(bands), BF16)


def _pool_kernel(prev_ref, x_ref, next_ref, band_ref, wp_ref, ps_ref, o_ref, *, lay, tm, sub):
    gw = x_ref.shape[1] // len(POOL_WINDOWS)
    groups = range(len(POOL_WINDOWS))
    cols = [slice(g * gw, (g + 1) * gw) for g in groups]

    def masked_rows(t):
        r0 = t * tm
        idx, cps = _chunk_in_seq((pl.program_id(0) * sub + t) * (tm // CHUNK), lay)
        p0 = idx * CHUNK
        seq_rows = cps * CHUNK
        pos_h = p0 - POOL_HALO + lax.broadcasted_iota(jnp.int32, (tm + 2 * POOL_HALO, 1), 0)
        valid_h = jnp.logical_and(pos_h >= META_PAD, pos_h < seq_rows)
        before = prev_ref[...] if t == 0 else x_ref[r0 - POOL_HALO:r0, :]
        after = next_ref[...] if t == sub - 1 else x_ref[r0 + tm:r0 + tm + POOL_HALO, :]
        x_all = jnp.concatenate([before, x_ref[r0:r0 + tm, :], after], axis=0)
        xm = jnp.where(valid_h, x_all, 0.0)
        hi = xm.astype(BF16)
        lo = (xm - hi.astype(F32)).astype(BF16)
        return dict(r0=r0, pos=p0 + lax.broadcasted_iota(jnp.int32, (tm, 1), 0), seq_rows=seq_rows,
                    xm=xm, hi=hi, lo=lo)

    tiles = [masked_rows(t) for t in range(sub)]
    for x in tiles:
        x["total"] = [_dot(band_ref[g], x["hi"][:, cols[g]]) + _dot(band_ref[g], x["lo"][:, cols[g]])
                      for g in groups]
    for x in tiles:
        x["pooled"] = []
        for g, window in enumerate(POOL_WINDOWS):
            back = window // 2
            count = (jnp.minimum(x["pos"] + (window - back), x["seq_rows"])
                     - jnp.maximum(x["pos"] - back, META_PAD))
            count = jnp.maximum(count, 1).astype(F32)
            centre = x["xm"][POOL_HALO:POOL_HALO + tm, cols[g]]
            x["pooled"].append((x["total"][g] / count - centre).astype(BF16))
    for x in tiles:
        x["y"] = [_dot(x["pooled"][g], wp_ref[g]) for g in groups]
    for x in tiles:
        valid = x["pos"] >= META_PAD
        for g in groups:
            y = jnp.where(valid, x["y"][g] * ps_ref[:, cols[g]], 0.0)
            o_ref[x["r0"]:x["r0"] + tm, cols[g]] = y.astype(o_ref.dtype)


def _pool(u, w_pool, layer, pool_scale, *, lay, tm, sub, hgw):
    rows = u.shape[0]
    pw = pool_scale.shape[1]
    col = 5 * hgw // pw
    bt = tm * sub
    hb = bt // POOL_HALO
    last_hb = rows // POOL_HALO - 1
    bands = _pool_bands(tm)
    kern = functools.partial(_pool_kernel, lay=lay, tm=tm, sub=sub)
    return pl.pallas_call(
        kern,
        grid=(rows // bt,),
        in_specs=[
            pl.BlockSpec((POOL_HALO, pw), lambda i: (jnp.maximum(i * hb - 1, 0), col)),
            pl.BlockSpec((bt, pw), lambda i: (i, col)),
            pl.BlockSpec((POOL_HALO, pw), lambda i: (jnp.minimum((i + 1) * hb, last_hb), col)),
            pl.BlockSpec(bands.shape, lambda i: (0, 0, 0)),
            pl.BlockSpec((None,) + w_pool.shape[1:], lambda i: (layer, 0, 0, 0)),
            pl.BlockSpec((1, pw), lambda i: (0, 0)),
        ],
        out_specs=pl.BlockSpec((bt, pw), lambda i: (i, 0)),
        out_shape=jax.ShapeDtypeStruct((rows, pw), BF16),
        compiler_params=_params(("parallel",)),
        name="pool_mixer",
    )(u, u, u, bands, w_pool, pool_scale)


def _mlp_kernel(h_ref, g_ref, wu_ref, wd_ref, *rest, final, f_axis):
    o_ref, m_scr = rest[-2:]
    f = pl.program_id(f_axis)

    def contribution(m):
        hidden = jnp.square(jnp.maximum(_dot(m, wu_ref[...]), 0.0)).astype(BF16)
        return _dot(hidden, wd_ref[...])

    @pl.when(f == 0)
    def _():
        h = h_ref[...]
        m = _rms(h, g_ref[...]).astype(BF16)
        m_scr[...] = m
        o_ref[...] = h + contribution(m)

    @pl.when(f > 0)
    def _():
        o_ref[...] += contribution(m_scr[...])

    if final:
        @pl.when(f == pl.num_programs(f_axis) - 1)
        def _():
            o_ref[...] = _rms(o_ref[...], rest[0][...])


def _mlp(h, gain, w_up, w_down, layer, *, bm, bf):
    rows, d = h.shape
    dff = w_up.shape[2]
    return pl.pallas_call(
        functools.partial(_mlp_kernel, final=False, f_axis=1),
        grid=(rows // bm, dff // bf),
        in_specs=[
            pl.BlockSpec((bm, d), lambda i, f: (i, 0)),
            pl.BlockSpec((1, d), lambda i, f: (0, 0)),
            pl.BlockSpec((None, d, bf), lambda i, f: (layer, 0, f)),
            pl.BlockSpec((None, bf, d), lambda i, f: (layer, f, 0)),
        ],
        out_specs=pl.BlockSpec((bm, d), lambda i, f: (i, 0)),
        out_shape=jax.ShapeDtypeStruct((rows, d), F32),
        scratch_shapes=[pltpu.VMEM((bm, d), BF16)],
        compiler_params=_params(("parallel", "arbitrary")),
        name="mlp",
    )(h, gain, w_up, w_down)


def _mlp_final(h, gain, w_up, w_down, layer, final_gain, *, row0, n_seq, seq_rows, s, bm, bf):
    d = h.shape[1]
    dff = w_up.shape[2]
    skip = seq_rows - s
    return pl.pallas_call(
        functools.partial(_mlp_kernel, final=True, f_axis=2),
        grid=(n_seq, s // bm, dff // bf),
        in_specs=[
            pl.BlockSpec((pl.Element(bm), pl.Element(d)),
                         lambda b, i, f: (pl.multiple_of(row0 + b * seq_rows + skip + i * bm, CHUNK), 0)),
            pl.BlockSpec((1, d), lambda b, i, f: (0, 0)),
            pl.BlockSpec((None, d, bf), lambda b, i, f: (layer, 0, f)),
            pl.BlockSpec((None, bf, d), lambda b, i, f: (layer, f, 0)),
            pl.BlockSpec((1, d), lambda b, i, f: (0, 0)),
        ],
        out_specs=pl.BlockSpec((None, bm, d), lambda b, i, f: (b, i, 0)),
        out_shape=jax.ShapeDtypeStruct((n_seq, s, d), F32),
        scratch_shapes=[pltpu.VMEM((bm, d), BF16)],
        compiler_params=_params(("parallel", "parallel", "arbitrary")),
        name="mlp_final",
    )(h, gain, w_up, w_down, final_gain)


def _pack_rows(xs, meta):
    d = meta.shape[1]
    lead = jnp.concatenate([jnp.zeros((META_PAD, d), F32), meta.astype(F32)], axis=0)
    pieces = []
    for x in xs:
        for b in range(x.shape[0]):
            pieces += [lead, x[b]]
    return jnp.concatenate(pieces, axis=0)


def _tile_plan(lay):
    nc = _num_chunks(lay)
    g = 1
    for d in range(1, min(lay.cpa, lay.cpb) + 1):
        if lay.cpa % d == 0 and lay.cpb % d == 0:
            g = d
    return dict(
        proj_bm=CHUNK * _largest_divisor(nc, 13),
        bwd_cpt=_largest_divisor(nc, 12),
        scan_cpt=_largest_divisor(nc, 6),
        pool_tm=CHUNK * _largest_divisor(g, 4),
        pool_sub=_largest_divisor(nc // _largest_divisor(g, 4), 4),
        mlp_bm=CHUNK * _largest_divisor(nc, 10),
    )


def kernel(x_prompt, x_sample, meta_tokens, w_in, w_pool, pool_scale, hg_lower_bound, hg_head_norm,
           w_out, norm_mix, norm_mlp, w_up, w_down, final_norm):
    depth, d, in_cols = w_in.shape
    pw = pool_scale.shape[1]
    hgw = hg_head_norm.shape[1]
    dff = w_up.shape[2]
    lead = N_META + META_PAD
    s_a, s_b = x_prompt.shape[1], x_sample.shape[1]
    assert in_cols == 5 * hgw + pw and hgw % HEAD == 0 and pw == hgw
    assert s_a % CHUNK == 0 and s_b % CHUNK == 0
    lay = Layout(x_prompt.shape[0], (s_a + lead) // CHUNK, x_sample.shape[0], (s_b + lead) // CHUNK)
    plan = _tile_plan(lay)
    proj_bn = in_cols // _largest_divisor(in_cols // 256, 3)
    mlp_bf = dff // _largest_divisor(dff // 256, 8)

    probs = jax.nn.softmax(hg_lower_bound.astype(F32), axis=1)
    lower = jnp.cumsum(probs, axis=1) - probs[:, :1]

    h = _pack_rows((x_prompt, x_sample), meta_tokens)
    row = lambda a: a.astype(F32).reshape(1, -1)
    w_in, w_pool, w_out, w_up, w_down = (w.astype(BF16) for w in (w_in, w_pool, w_out, w_up, w_down))
    for l in range(depth):
        u = _norm_proj(h, row(norm_mix[l]), w_in, l, bm=plan["proj_bm"], bn=proj_bn)
        sb = _bwd_states(u, lower[1:2, l], lay=lay, cpt=plan["bwd_cpt"], hgw=hgw)
        y_pool = _pool(u, w_pool, l, row(pool_scale[l]), lay=lay, tm=plan["pool_tm"],
                       sub=plan["pool_sub"], hgw=hgw)
        h = _hgrn2_out(u, sb, lower[:, l], row(hg_head_norm[l]), h, y_pool, w_out, l,
                       lay=lay, cpt=plan["scan_cpt"], hgw=hgw)
        mlp_w = (row(norm_mlp[l]), w_up, w_down, l)
        if l < depth - 1:
            h = _mlp(h, *mlp_w, bm=plan["mlp_bm"], bf=mlp_bf)

    def final(row0, n_seq, s):
        bm = CHUNK * _largest_divisor(s // CHUNK, 8)
        return _mlp_final(h, *mlp_w, row(final_norm), row0=row0, n_seq=n_seq, seq_rows=s + lead,
                          s=s, bm=bm, bf=2 * mlp_bf)

    return (final(0, lay.n_a, s_a), final(lay.n_a * lay.cpa * CHUNK, lay.n_b, s_b))
```

```python
import collections
import functools

import jax
import jax.numpy as jnp
import numpy as np
from jax import lax
from jax.experimental import pallas as pl
from jax.experimental.pallas import tpu as pltpu

N_META = 16
CHUNK = 64
META_PAD = (-N_META) % CHUNK
HEAD = 128
SUBLANES = 8
POOL_WINDOWS = (2, 4, 8, 16)
POOL_HALO = 8
OUT_PIECE = 256
EPS = 1e-6
FORGET_FLOOR = 1e-30
SAFE_LOG2_DECAY = 86.0
CLEAR_LOG2_DECAY = -1e30

VMEM_LIMIT_BYTES = 56 * 1024 * 1024

F32 = jnp.float32
BF16 = jnp.bfloat16

Layout = collections.namedtuple("Layout", "n_a cpa n_b cpb")


def _num_chunks(lay):
    return lay.n_a * lay.cpa + lay.n_b * lay.cpb


def _chunk_in_seq(cg, lay):
    na = lay.n_a * lay.cpa
    in_a = cg < na
    idx = jnp.where(in_a, lax.rem(cg, lay.cpa), lax.rem(jnp.maximum(cg - na, 0), lay.cpb))
    cps = jnp.where(in_a, lay.cpa, lay.cpb)
    return idx, cps


def _largest_divisor(n, cap):
    best = 1
    for d in range(1, n + 1):
        if n % d == 0 and d <= cap:
            best = d
    return best


def _rms(x, gain):
    ms = jnp.mean(x * x, axis=-1, keepdims=True)
    return x * lax.rsqrt(ms + EPS) * gain


def _sigmoid(x):
    return 1.0 / (1.0 + jnp.exp(-x))


def _silu(x):
    hx = 0.5 * x
    return hx * jnp.tanh(hx) + hx


def _dot(a, b):
    return jnp.dot(a, b, preferred_element_type=F32)


def _dot_nt(a, b):
    return lax.dot_general(a, b, (((1,), (1,)), ((), ())), preferred_element_type=F32)


def _dot_tn(a, b):
    return lax.dot_general(a, b, (((0,), (0,)), ((), ())), preferred_element_type=F32)


def _params(sem):
    return pltpu.CompilerParams(dimension_semantics=sem, vmem_limit_bytes=VMEM_LIMIT_BYTES)


def _norm_proj_kernel(h_ref, g_ref, w_ref, o_ref, a_scr):
    @pl.when(pl.program_id(1) == 0)
    def _():
        a = _rms(h_ref[...], g_ref[...]).astype(BF16)
        a_scr[...] = a
        o_ref[...] = _dot(a, w_ref[...])

    @pl.when(pl.program_id(1) > 0)
    def _():
        o_ref[...] = _dot(a_scr[...], w_ref[...])


def _norm_proj(h, gain, w, layer, *, bm, bn):
    rows, d = h.shape
    n = w.shape[2]
    return pl.pallas_call(
        _norm_proj_kernel,
        grid=(rows // bm, n // bn),
        in_specs=[
            pl.BlockSpec((bm, d), lambda i, j: (i, 0)),
            pl.BlockSpec((1, d), lambda i, j: (0, 0)),
            pl.BlockSpec((None, d, bn), lambda i, j: (layer, 0, j)),
        ],
        out_specs=pl.BlockSpec((bm, bn), lambda i, j: (i, j)),
        out_shape=jax.ShapeDtypeStruct((rows, n), F32),
        scratch_shapes=[pltpu.VMEM((bm, d), BF16)],
        compiler_params=_params(("parallel", "arbitrary")),
        name="norm_proj",
    )(h, gain, w)


def _forget(f_pre, lb):
    span = 1.0 - lb
    w = span * _sigmoid(f_pre)
    log2_f = jnp.log2(jnp.maximum(lb + w, FORGET_FLOOR))
    return log2_f, span - w


def _tri(lower):
    r = lax.broadcasted_iota(jnp.int32, (CHUNK, CHUNK), 0)
    c = lax.broadcasted_iota(jnp.int32, (CHUNK, CHUNK), 1)
    return (r >= c) if lower else (r <= c)


def _cumsum_rows(tri_bf16, g):
    hi = g.astype(BF16)
    lo = (g - hi.astype(F32)).astype(BF16)
    return _dot(tri_bf16, hi) + _dot(tri_bf16, lo)


def _pad_row_mask(is_first):
    r = lax.broadcasted_iota(jnp.int32, (CHUNK, 1), 0)
    return jnp.logical_or(jnp.logical_not(is_first), r >= META_PAD)


def _bwd_state_kernel(fb_ref, ip_ref, lb_ref, sb_ref, st_scr, *, lay, cpt, heads):
    tile = pl.num_programs(0) - 1 - pl.program_id(0)
    upper = jnp.where(_tri(False), 1.0, 0.0).astype(BF16)
    lb = lb_ref[...]

    @pl.when(pl.program_id(0) == 0)
    def _():
        st_scr[...] = jnp.zeros_like(st_scr)

    group = 2 if heads % 2 == 0 else 1
    slabs = [(c, g) for c in reversed(range(cpt)) for g in range(heads // group)]

    def stage_gates(c, g):
        idx, _ = _chunk_in_seq(tile * cpt + c, lay)
        rows = slice(c * CHUNK, (c + 1) * CHUNK)
        cols = slice(g * group * HEAD, (g + 1) * group * HEAD)
        log2_f, k = _forget(fb_ref[rows, cols], lb[:, cols])
        return dict(c=c, g=g, is_first=idx == 0, k=k,
                    v16=ip_ref[rows, cols].astype(BF16),
                    cb=_cumsum_rows(upper, log2_f))

    def stage_operands(x):
        c_end = jnp.where(x["is_first"], CLEAR_LOG2_DECAY, x["cb"][0:1, :])
        x["khat"] = (x["k"] * jnp.exp2(c_end - x["cb"])).astype(BF16)
        x["decay"] = jnp.exp2(c_end)

    def stage_update(x):
        for j in range(group):
            h = x["g"] * group + j
            hs = slice(h * HEAD, (h + 1) * HEAD)
            hl = slice(j * HEAD, (j + 1) * HEAD)
            st_old = st_scr[:, hs]
            sb_ref[x["c"], :, hs] = st_old.astype(BF16)
            st_scr[:, hs] = st_old * x["decay"][:, hl] + _dot_tn(x["v16"][:, hl], x["khat"][:, hl])

    stages = (stage_operands, stage_update)
    in_flight = []
    for step in range(len(slabs) + len(stages)):
        in_flight.insert(0, stage_gates(*slabs[step]) if step < len(slabs) else None)
        in_flight = in_flight[:len(stages) + 1]
        for depth, stage in enumerate(stages, start=1):
            if depth < len(in_flight) and in_flight[depth] is not None:
                stage(in_flight[depth])


def _bwd_states(u, lb_b, *, lay, cpt, hgw):
    nc = _num_chunks(lay)
    nt = nc // cpt
    tr = cpt * CHUNK
    heads = hgw // HEAD
    kern = functools.partial(_bwd_state_kernel, lay=lay, cpt=cpt, heads=heads)
    return pl.pallas_call(
        kern,
        grid=(nt,),
        in_specs=[
            pl.BlockSpec((tr, hgw), lambda j: (nt - 1 - j, 2)),
            pl.BlockSpec((tr, hgw), lambda j: (nt - 1 - j, 3)),
            pl.BlockSpec((1, hgw), lambda j: (0, 0)),
        ],
        out_specs=pl.BlockSpec((cpt, HEAD, hgw), lambda j: (nt - 1 - j, 0, 0)),
        out_shape=jax.ShapeDtypeStruct((nc, HEAD, hgw), BF16),
        scratch_shapes=[pltpu.VMEM((HEAD, hgw), F32)],
        compiler_params=_params(("arbitrary",)),
        name="hgrn2_bwd_states",
    )(u, u, lb_b)


def _chunk_gates(q_ref, ff_ref, fb_ref, ip_ref, rows, cols, lb_f, lb_b, lower, upper):
    q = _silu(q_ref[rows, cols])
    v16 = ip_ref[rows, cols].astype(BF16)
    gf, kf = _forget(ff_ref[rows, cols], lb_f[:, cols])
    gb, kb = _forget(fb_ref[rows, cols], lb_b[:, cols])
    bf = _cumsum_rows(lower, gf)
    cb = _cumsum_rows(upper, gb)
    return q, v16, kf, bf, kb, cb


def _head_output(o, gate, head_norm, valid, dtype):
    ms = jnp.mean(o * o, axis=-1, keepdims=True)
    y = o * lax.rsqrt(ms + EPS) * head_norm * gate
    return jnp.where(valid, y, 0.0).astype(dtype)


def _exact_scores(q_scr, k_scr, b_scr, hs, mask):
    qh = q_scr[:, hs]
    bh = b_scr[:, hs]
    lane = lax.broadcasted_iota(jnp.int32, (CHUNK, CHUNK), 1)

    def body(sg, acc):
        group = pl.ds(pl.multiple_of(sg * SUBLANES, SUBLANES), SUBLANES)
        k8 = k_scr[group, hs]
        b8 = b_scr[group, hs]
        for j in range(SUBLANES):
            e = jnp.exp2(jnp.minimum(bh - b8[j:j + 1, :], 0.0))
            col = jnp.sum(qh * k8[j:j + 1, :] * e, axis=1, keepdims=True)
            acc = jnp.where(lane == sg * SUBLANES + j, col, acc)
        return acc

    acc = lax.fori_loop(0, CHUNK // SUBLANES, body, jnp.zeros((CHUNK, CHUNK), F32))
    return jnp.where(mask, acc, 0.0)


def _hgrn2_kernel(q_ref, ff_ref, fb_ref, ip_ref, gt_ref, sb_ref, lb_ref, hn_ref, h_ref, yp_ref, w_ref,
                  h1_ref, st_scr, oi_scr, redo_ref, y_scr, q_scr, kf_scr, bf_scr, kb_scr, cb_scr,
                  *, lay, cpt, heads):
    step = pl.program_id(0)
    tile = jnp.minimum(step, pl.num_programs(0) - 2)
    slot = lax.rem(step, 2)
    hgw = heads * HEAD
    lower_m = _tri(True)
    upper_m = _tri(False)
    lower = jnp.where(lower_m, 1.0, 0.0).astype(BF16)
    upper = jnp.where(upper_m, 1.0, 0.0).astype(BF16)
    lb_f = lb_ref[0:1, :]
    lb_b = lb_ref[1:2, :]
    hn = hn_ref[...]
    half = CHUNK // 2

    @pl.when(step == 0)
    def _():
        st_scr[...] = jnp.zeros_like(st_scr)
        y_scr[1] = jnp.zeros(y_scr.shape[1:], y_scr.dtype)

    group = 2 if heads % 2 == 0 else 1
    slabs = [(c, g) for c in range(cpt) for g in range(heads // group)]
    margin = {}

    def stage_gates(c, g):
        idx, cps = _chunk_in_seq(tile * cpt + c, lay)
        rows = slice(c * CHUNK, (c + 1) * CHUNK)
        cols = slice(g * group * HEAD, (g + 1) * group * HEAD)
        q, v16, kf, bf, kb, cb = _chunk_gates(q_ref, ff_ref, fb_ref, ip_ref, rows, cols,
                                              lb_f, lb_b, lower, upper)
        return dict(c=c, g=g, rows=rows, valid=_pad_row_mask(idx == 0), is_last=idx == cps - 1,
                    q=q, v16=v16, kf=kf, bf=bf, kb=kb, cb=cb, gate=_silu(gt_ref[rows, cols]))

    def stage_operands(x):
        q, kf, bf, kb, cb = x["q"], x["kf"], x["bf"], x["kb"], x["cb"]
        b_last = bf[CHUNK - 1:CHUNK, :]
        c_first = cb[0:1, :]
        rf = bf[half - 1:half, :]
        rb = cb[half:half + 1, :]
        ef = jnp.exp2(bf - rf)
        eb = jnp.exp2(cb - rb)
        b_end = jnp.where(x["is_last"], CLEAR_LOG2_DECAY, b_last)
        q_big = jnp.log2(jnp.maximum(jnp.max(jnp.abs(q), axis=0, keepdims=True), 1.0))
        m = jnp.minimum(jnp.minimum(rf - q_big, b_last - rf), jnp.minimum(rb - q_big, c_first - rb))
        margin[x["c"]] = m if x["g"] == 0 else jnp.minimum(margin[x["c"]], m)
        if x["g"] == heads // group - 1:
            redo_ref[x["c"]] = jnp.where(jnp.min(margin[x["c"]]) < -SAFE_LOG2_DECAY, 1, 0)
        x.update(
            qtf=(q * ef).astype(BF16),
            ktf=(kf * (1.0 / ef)).astype(BF16),
            qtb=(q * eb).astype(BF16),
            ktb=(kb * (1.0 / eb)).astype(BF16),
            qhf=(q * jnp.exp2(bf)).astype(BF16),
            qhb=(q * jnp.exp2(cb)).astype(BF16),
            khat=(kf * jnp.exp2(b_end - bf)).astype(BF16),
            decay=jnp.exp2(b_end),
        )

    def group_heads(x):
        for j in range(group):
            h = x["g"] * group + j
            yield j, slice(h * HEAD, (h + 1) * HEAD), slice(j * HEAD, (j + 1) * HEAD)

    def stage_scores(x):
        x["af"], x["ab"], x["oi"] = {}, {}, {}
        for j, hs, hl in group_heads(x):
            st_old = st_scr[:, hs]
            x["af"][j] = _dot_nt(x["qtf"][:, hl], x["ktf"][:, hl])
            x["ab"][j] = _dot_nt(x["qtb"][:, hl], x["ktb"][:, hl])
            qcat = jnp.concatenate([x["qhf"][:, hl], x["qhb"][:, hl]], axis=1)
            scat = jnp.concatenate([st_old.astype(BF16), sb_ref[x["c"], :, hs]], axis=1)
            x["oi"][j] = _dot_nt(qcat, scat)
            st_scr[:, hs] = st_old * x["decay"][:, hl] + _dot_tn(x["v16"][:, hl], x["khat"][:, hl])

    def stage_mix(x):
        x["o"] = {}
        for j, hs, hl in group_heads(x):
            a = (jnp.where(lower_m, x["af"][j], 0.0) + jnp.where(upper_m, x["ab"][j], 0.0)).astype(BF16)
            oi_scr[x["rows"], hs] = x["oi"][j]
            x["o"][j] = _dot(a, x["v16"][:, hl]) + x["oi"][j]

    def stage_output(x):
        for j, hs, hl in group_heads(x):
            y_scr[slot, x["rows"], hs] = _head_output(x["o"][j], x["gate"][:, hl], hn[:, hs], x["valid"],
                                                      y_scr.dtype)

    def chunk_redo(c):
        idx, _ = _chunk_in_seq(tile * cpt + c, lay)
        rows = pl.ds(pl.multiple_of(c * CHUNK, CHUNK), CHUNK)
        valid = _pad_row_mask(idx == 0)
        q, v16, kf, bf, kb, cb = _chunk_gates(q_ref, ff_ref, fb_ref, ip_ref, rows, slice(None),
                                              lb_f, lb_b, lower, upper)
        q_scr[...] = q
        kf_scr[...] = kf
        bf_scr[...] = bf
        kb_scr[...] = kb
        cb_scr[...] = cb
        gate = _silu(gt_ref[rows, :])
        for h in range(heads):
            hs = slice(h * HEAD, (h + 1) * HEAD)
            a = (_exact_scores(q_scr, kf_scr, bf_scr, hs, lower_m)
                 + _exact_scores(q_scr, kb_scr, cb_scr, hs, upper_m)).astype(BF16)
            o = _dot(a, v16[:, hs]) + oi_scr[rows, hs]
            y_scr[slot, rows, hs] = _head_output(o, gate[:, hs], hn[:, hs], valid, y_scr.dtype)

    def project_piece(n):
        cols = slice(n * OUT_PIECE, (n + 1) * OUT_PIECE)
        h1_ref[:, cols] = (h_ref[:, cols] + _dot(y_scr[1 - slot], w_ref[0:hgw, cols])
                           + _dot(yp_ref[...], w_ref[hgw:, cols]))

    pieces = h1_ref.shape[1] // OUT_PIECE
    every = max(1, len(slabs) // pieces)
    stages = (stage_operands, stage_scores, stage_mix, stage_output)
    in_flight = []
    issued = 0
    for k in range(len(slabs) + len(stages)):
        in_flight.insert(0, stage_gates(*slabs[k]) if k < len(slabs) else None)
        in_flight = in_flight[:len(stages) + 1]
        for depth, stage in enumerate(stages, start=1):
            if depth < len(in_flight) and in_flight[depth] is not None:
                stage(in_flight[depth])
        if k % every == 0 and issued < pieces:
            project_piece(issued)
            issued += 1
    for n in range(issued, pieces):
        project_piece(n)

    def redo_body(c, carry):
        @pl.when(redo_ref[c] != 0)
        def _():
            chunk_redo(c)
        return carry

    lax.fori_loop(0, cpt, redo_body, 0)


def _hgrn2_out(u, sb, lb, head_norm, h, y_pool, w_out, layer, *, lay, cpt, hgw):
    nc = _num_chunks(lay)
    nt = nc // cpt
    tr = cpt * CHUNK
    heads = hgw // HEAD
    d = h.shape[1]
    pw = y_pool.shape[1]
    kern = functools.partial(_hgrn2_kernel, lay=lay, cpt=cpt, heads=heads)
    scan_tile = lambda i: jnp.minimum(i, nt - 1)
    proj_tile = lambda i: jnp.maximum(i - 1, 0)
    col = lambda part: pl.BlockSpec((tr, hgw), lambda i: (scan_tile(i), part))
    chunk_f32 = pltpu.VMEM((CHUNK, hgw), F32)
    return pl.pallas_call(
        kern,
        grid=(nt + 1,),
        in_specs=[
            col(0), col(1), col(2), col(3), col(4),
            pl.BlockSpec((cpt, HEAD, hgw), lambda i: (scan_tile(i), 0, 0)),
            pl.BlockSpec((2, hgw), lambda i: (0, 0)),
            pl.BlockSpec((1, hgw), lambda i: (0, 0)),
            pl.BlockSpec((tr, d), lambda i: (proj_tile(i), 0)),
            pl.BlockSpec((tr, pw), lambda i: (proj_tile(i), 0)),
            pl.BlockSpec((None,) + w_out.shape[1:], lambda i: (layer, 0, 0), pipeline_mode=pl.Buffered(1)),
        ],
        out_specs=pl.BlockSpec((tr, d), lambda i: (proj_tile(i), 0)),
        out_shape=jax.ShapeDtypeStruct(h.shape, F32),
        scratch_shapes=[
            pltpu.VMEM((HEAD, hgw), F32),
            pltpu.VMEM((tr, hgw), F32),
            pltpu.SMEM((cpt,), jnp.int32),
            pltpu.VMEM((2, tr, hgw), BF16),
            chunk_f32, chunk_f32, chunk_f32, chunk_f32, chunk_f32,
        ],
        compiler_params=_params(("arbitrary",)),
        name="hgrn2_scan_out_proj",
    )(u, u, u, u, u, sb, lb, head_norm, h, y_pool, w_out)


def _pool_bands(tm):
    r = np.arange(tm)[:, None]
    j = np.arange(tm + 2 * POOL_HALO)[None, :] - POOL_HALO
    bands = [(j >= r - w // 2) & (j < r + w - w // 2) for w in POOL_WINDOWS]
    return jnp.asarray(np.stack(bands), BF16)


def _pool_kernel(prev_ref, x_ref, next_ref, band_ref, wp_ref, ps_ref, o_ref, *, lay, tm, sub):
    gw = x_ref.shape[1] // len(POOL_WINDOWS)
    groups = range(len(POOL_WINDOWS))
    cols = [slice(g * gw, (g + 1) * gw) for g in groups]

    def masked_rows(t):
        r0 = t * tm
        idx, cps = _chunk_in_seq((pl.program_id(0) * sub + t) * (tm // CHUNK), lay)
        p0 = idx * CHUNK
        seq_rows = cps * CHUNK
        pos_h = p0 - POOL_HALO + lax.broadcasted_iota(jnp.int32, (tm + 2 * POOL_HALO, 1), 0)
        valid_h = jnp.logical_and(pos_h >= META_PAD, pos_h < seq_rows)
        before = prev_ref[...] if t == 0 else x_ref[r0 - POOL_HALO:r0, :]
        after = next_ref[...] if t == sub - 1 else x_ref[r0 + tm:r0 + tm + POOL_HALO, :]
        x_all = jnp.concatenate([before, x_ref[r0:r0 + tm, :], after], axis=0)
        xm = jnp.where(valid_h, x_all, 0.0)
        hi = xm.astype(BF16)
        lo = (xm - hi.astype(F32)).astype(BF16)
        return dict(r0=r0, pos=p0 + lax.broadcasted_iota(jnp.int32, (tm, 1), 0), seq_rows=seq_rows,
                    xm=xm, hi=hi, lo=lo)

    tiles = [masked_rows(t) for t in range(sub)]
    for x in tiles:
        x["total"] = [_dot(band_ref[g], x["hi"][:, cols[g]]) + _dot(band_ref[g], x["lo"][:, cols[g]])
                      for g in groups]
    for x in tiles:
        x["pooled"] = []
        for g, window in enumerate(POOL_WINDOWS):
            back = window // 2
            count = (jnp.minimum(x["pos"] + (window - back), x["seq_rows"])
                     - jnp.maximum(x["pos"] - back, META_PAD))
            count = jnp.maximum(count, 1).astype(F32)
            centre = x["xm"][POOL_HALO:POOL_HALO + tm, cols[g]]
            x["pooled"].append((x["total"][g] / count - centre).astype(BF16))
    for x in tiles:
        x["y"] = [_dot(x["pooled"][g], wp_ref[g]) for g in groups]
    for x in tiles:
        valid = x["pos"] >= META_PAD
        for g in groups:
            y = jnp.where(valid, x["y"][g] * ps_ref[:, cols[g]], 0.0)
            o_ref[x["r0"]:x["r0"] + tm, cols[g]] = y.astype(o_ref.dtype)


def _pool(u, w_pool, layer, pool_scale, *, lay, tm, sub, hgw):
    rows = u.shape[0]
    pw = pool_scale.shape[1]
    col = 5 * hgw // pw
    bt = tm * sub
    hb = bt // POOL_HALO
    last_hb = rows // POOL_HALO - 1
    bands = _pool_bands(tm)
    kern = functools.partial(_pool_kernel, lay=lay, tm=tm, sub=sub)
    return pl.pallas_call(
        kern,
        grid=(rows // bt,),
        in_specs=[
            pl.BlockSpec((POOL_HALO, pw), lambda i: (jnp.maximum(i * hb - 1, 0), col)),
            pl.BlockSpec((bt, pw), lambda i: (i, col)),
            pl.BlockSpec((POOL_HALO, pw), lambda i: (jnp.minimum((i + 1) * hb, last_hb), col)),
            pl.BlockSpec(bands.shape, lambda i: (0, 0, 0)),
            pl.BlockSpec((None,) + w_pool.shape[1:], lambda i: (layer, 0, 0, 0)),
            pl.BlockSpec((1, pw), lambda i: (0, 0)),
        ],
        out_specs=pl.BlockSpec((bt, pw), lambda i: (i, 0)),
        out_shape=jax.ShapeDtypeStruct((rows, pw), BF16),
        compiler_params=_params(("parallel",)),
        name="pool_mixer",
    )(u, u, u, bands, w_pool, pool_scale)


def _mlp_kernel(h_ref, g_ref, wu_ref, wd_ref, *rest, final, f_axis):
    o_ref, m_scr = rest[-2:]
    f = pl.program_id(f_axis)

    def contribution(m):
        half = wu_ref.shape[1] // 2
        total = None
        for lo in (0, half):
            hidden = jnp.square(jnp.maximum(_dot(m, wu_ref[:, lo:lo + half]), 0.0)).astype(BF16)
            part = _dot(hidden, wd_ref[lo:lo + half, :])
            total = part if total is None else total + part
        return total

    @pl.when(f == 0)
    def _():
        h = h_ref[...]
        m = _rms(h, g_ref[...]).astype(BF16)
        m_scr[...] = m
        o_ref[...] = h + contribution(m)

    @pl.when(f > 0)
    def _():
        o_ref[...] += contribution(m_scr[...])

    if final:
        @pl.when(f == pl.num_programs(f_axis) - 1)
        def _():
            o_ref[...] = _rms(o_ref[...], rest[0][...])


def _mlp(h, gain, w_up, w_down, layer, *, bm, bf):
    rows, d = h.shape
    dff = w_up.shape[2]
    return pl.pallas_call(
        functools.partial(_mlp_kernel, final=False, f_axis=1),
        grid=(rows // bm, dff // bf),
        in_specs=[
            pl.BlockSpec((bm, d), lambda i, f: (i, 0)),
            pl.BlockSpec((1, d), lambda i, f: (0, 0)),
            pl.BlockSpec((None, d, bf), lambda i, f: (layer, 0, f)),
            pl.BlockSpec((None, bf, d), lambda i, f: (layer, f, 0)),
        ],
        out_specs=pl.BlockSpec((bm, d), lambda i, f: (i, 0)),
        out_shape=jax.ShapeDtypeStruct((rows, d), F32),
        scratch_shapes=[pltpu.VMEM((bm, d), BF16)],
        compiler_params=_params(("parallel", "arbitrary")),
        name="mlp",
    )(h, gain, w_up, w_down)


def _mlp_final(h, gain, w_up, w_down, layer, final_gain, *, row0, n_seq, seq_rows, s, bm, bf):
    d = h.shape[1]
    dff = w_up.shape[2]
    skip = seq_rows - s
    return pl.pallas_call(
        functools.partial(_mlp_kernel, final=True, f_axis=2),
        grid=(n_seq, s // bm, dff // bf),
        in_specs=[
            pl.BlockSpec((pl.Element(bm), pl.Element(d)),
                         lambda b, i, f: (pl.multiple_of(row0 + b * seq_rows + skip + i * bm, CHUNK), 0)),
            pl.BlockSpec((1, d), lambda b, i, f: (0, 0)),
            pl.BlockSpec((None, d, bf), lambda b, i, f: (layer, 0, f)),
            pl.BlockSpec((None, bf, d), lambda b, i, f: (layer, f, 0)),
            pl.BlockSpec((1, d), lambda b, i, f: (0, 0)),
        ],
        out_specs=pl.BlockSpec((None, bm, d), lambda b, i, f: (b, i, 0)),
        out_shape=jax.ShapeDtypeStruct((n_seq, s, d), F32),
        scratch_shapes=[pltpu.VMEM((bm, d), BF16)],
        compiler_params=_params(("parallel", "parallel", "arbitrary")),
        name="mlp_final",
    )(h, gain, w_up, w_down, final_gain)


def _pack_rows(xs, meta):
    d = meta.shape[1]
    lead = jnp.concatenate([jnp.zeros((META_PAD, d), F32), meta.astype(F32)], axis=0)
    pieces = []
    for x in xs:
        for b in range(x.shape[0]):
            pieces += [lead, x[b]]
    return jnp.concatenate(pieces, axis=0)


def _tile_plan(lay):
    nc = _num_chunks(lay)
    g = 1
    for d in range(1, min(lay.cpa, lay.cpb) + 1):
        if lay.cpa % d == 0 and lay.cpb % d == 0:
            g = d
    return dict(
        proj_bm=CHUNK * _largest_divisor(nc, 13),
        bwd_cpt=_largest_divisor(nc, 12),
        scan_cpt=_largest_divisor(nc, 6),
        pool_tm=CHUNK * _largest_divisor(g, 4),
        pool_sub=_largest_divisor(nc // _largest_divisor(g, 4), 4),
        mlp_bm=CHUNK * _largest_divisor(nc, 10),
    )


def kernel(x_prompt, x_sample, meta_tokens, w_in, w_pool, pool_scale, hg_lower_bound, hg_head_norm,
           w_out, norm_mix, norm_mlp, w_up, w_down, final_norm):
    depth, d, in_cols = w_in.shape
    pw = pool_scale.shape[1]
    hgw = hg_head_norm.shape[1]
    dff = w_up.shape[2]
    lead = N_META + META_PAD
    s_a, s_b = x_prompt.shape[1], x_sample.shape[1]
    assert in_cols == 5 * hgw + pw and hgw % HEAD == 0 and pw == hgw
    assert s_a % CHUNK == 0 and s_b % CHUNK == 0
    lay = Layout(x_prompt.shape[0], (s_a + lead) // CHUNK, x_sample.shape[0], (s_b + lead) // CHUNK)
    plan = _tile_plan(lay)
    proj_bn = in_cols // _largest_divisor(in_cols // 256, 3)
    mlp_bf = dff // _largest_divisor(dff // 256, 8)

    probs = jax.nn.softmax(hg_lower_bound.astype(F32), axis=1)
    lower = jnp.cumsum(probs, axis=1) - probs[:, :1]

    h = _pack_rows((x_prompt, x_sample), meta_tokens)
    row = lambda a: a.astype(F32).reshape(1, -1)
    w_in, w_pool, w_out, w_up, w_down = (w.astype(BF16) for w in (w_in, w_pool, w_out, w_up, w_down))
    for l in range(depth):
        u = _norm_proj(h, row(norm_mix[l]), w_in, l, bm=plan["proj_bm"], bn=proj_bn)
        sb = _bwd_states(u, lower[1:2, l], lay=lay, cpt=plan["bwd_cpt"], hgw=hgw)
        y_pool = _pool(u, w_pool, l, row(pool_scale[l]), lay=lay, tm=plan["pool_tm"],
                       sub=plan["pool_sub"], hgw=hgw)
        h = _hgrn2_out(u, sb, lower[:, l], row(hg_head_norm[l]), h, y_pool, w_out, l,
                       lay=lay, cpt=plan["scan_cpt"], hgw=hgw)
        mlp_w = (row(norm_mlp[l]), w_up, w_down, l)
        if l < depth - 1:
            h = _mlp(h, *mlp_w, bm=plan["mlp_bm"], bf=mlp_bf)

    def final(row0, n_seq, s):
        bm = CHUNK * _largest_divisor(s // CHUNK, 8)
        return _mlp_final(h, *mlp_w, row(final_norm), row0=row0, n_seq=n_seq, seq_rows=s + lead,
                          s=s, bm=bm, bf=2 * mlp_bf)

    return (final(0, lay.n_a, s_a), final(lay.n_a * lay.cpa * CHUNK, lay.n_b, s_b))
```
